```python
import math
import jax
import jax.numpy as jnp
from jax import lax
import numpy as np

D_MODEL = 2048
BATCH = 8
SEQ = 4096
DEPTH = 1
DEC_BATCH = 1
DEC_SEQ = 16384
PAST_LEN = 128

HEAD_DIM = 128
N_SLOTS = 8
ATTN_GROUPS = ((128, 1), (512, 4), (2048, 16))
N_GROUPS = 3
ATTN_WIDTH = N_SLOTS * HEAD_DIM
QKV_WIDTH = N_GROUPS * ATTN_WIDTH
ROT_DIM = HEAD_DIM // 4
ROPE_THETA = 500000.0
HYENA_WIDTH = 1024
SHORT_CONV = 3
FILTER_EMB = 33
FILTER_HIDDEN = 64
FILTER_OUT_SCALE = 0.05
DECAY_TARGET = 1e-2
DECAY_FAST_PCT = 0.3
DECAY_SLOW_PCT = 1.5
RMS_EPS = 1e-6
IN_WIDTH = 3 * QKV_WIDTH + ATTN_WIDTH + 4 * HYENA_WIDTH + 2 * D_MODEL
SPLIT_POINTS = (
    QKV_WIDTH,
    2 * QKV_WIDTH,
    3 * QKV_WIDTH,
    3 * QKV_WIDTH + ATTN_WIDTH,
    3 * QKV_WIDTH + ATTN_WIDTH + 3 * HYENA_WIDTH,
    3 * QKV_WIDTH + ATTN_WIDTH + 4 * HYENA_WIDTH,
    3 * QKV_WIDTH + ATTN_WIDTH + 4 * HYENA_WIDTH + D_MODEL,
)

kernel_name = "dilated_attn_hyena_gated_merge_encoder"


def rms_norm(x, gain):
    xf = x.astype(jnp.float32)
    y = xf * lax.rsqrt(jnp.mean(xf * xf, axis=-1, keepdims=True) + RMS_EPS)
    return (y * gain.astype(jnp.float32)).astype(x.dtype)


def rope_tables(L):
    inv_freq = jnp.power(ROPE_THETA, -jnp.arange(0, ROT_DIM, 2, dtype=jnp.float32) / ROT_DIM)
    ang = jnp.arange(L, dtype=jnp.float32)[:, None] * inv_freq[None, :]
    return jnp.cos(ang), jnp.sin(ang)


def apply_partial_rope(t, cos, sin):
    half = ROT_DIM // 2
    c = cos[None, :, None, None, :].astype(t.dtype)
    s = sin[None, :, None, None, :].astype(t.dtype)
    t1 = t[..., :half]
    t2 = t[..., half:ROT_DIM]
    return jnp.concatenate([t1 * c - t2 * s, t2 * c + t1 * s, t[..., ROT_DIM:]], axis=-1)


def dilated_window_attention(q, k, v, window, dilation):
    B, L, H, E = q.shape
    radius = window // (2 * dilation)
    blk = radius
    n = L // dilation
    nb = -(-n // blk)
    n_pad = nb * blk

    def to_classes(t):
        t = t.reshape(B, n, dilation, H, E).transpose(0, 2, 1, 3, 4)
        return jnp.pad(t, ((0, 0), (0, 0), (0, n_pad - n), (0, 0), (0, 0)))

    qc, kc, vc = to_classes(q), to_classes(k), to_classes(v)
    qb = qc.reshape(B, dilation, nb, blk, H, E)

    def windows(t):
        t = jnp.pad(t.reshape(B, dilation, nb, blk, H, E),
                    ((0, 0), (0, 0), (1, 1), (0, 0), (0, 0), (0, 0)))
        return jnp.concatenate([t[:, :, :-2], t[:, :, 1:-1], t[:, :, 2:]], axis=3)

    kw, vw = windows(kc), windows(vc)
    scores = jnp.einsum('bdnqhe,bdnkhe->bdnhqk', qb, kw).astype(jnp.float32) / math.sqrt(E)
    qpos = jnp.arange(nb)[:, None, None] * blk + jnp.arange(blk)[None, :, None]
    kpos = (jnp.arange(nb)[:, None, None] - 1) * blk + jnp.arange(3 * blk)[None, None, :]
    valid = (jnp.abs(kpos - qpos) <= radius) & (kpos >= 0) & (kpos < n)
    scores = jnp.where(valid[:, None], scores, -1e30)
    m = jnp.max(scores, axis=-1, keepdims=True)
    p = jnp.exp(scores - m)
    denom = jnp.sum(p, axis=-1, keepdims=True)
    out = jnp.einsum('bdnhqk,bdnkhe->bdnqhe', p / denom, vw.astype(jnp.float32))
    lse = jnp.swapaxes((m + jnp.log(denom))[..., 0], 3, 4)
    out = out.reshape(B, dilation, n_pad, H, E)[:, :, :n].transpose(0, 2, 1, 3, 4).reshape(B, L, H, E)
    lse = lse.reshape(B, dilation, n_pad, H)[:, :, :n].transpose(0, 2, 1, 3).reshape(B, L, H)
    return out, lse


def short_conv(u, w, b):
    up = jnp.pad(u, ((0, 0), (1, 1), (0, 0)))
    return up[:, :-2] * w[0] + up[:, 1:-1] * w[1] + up[:, 2:] * w[2] + b


def implicit_filters(L, w1, b1, w2, b2, w3, b3, w4, freq):
    bands = (FILTER_EMB - 1) // 2
    t = jnp.linspace(0.0, 1.0, L, dtype=jnp.float32)[:, None]
    w = 2.0 * math.pi * jnp.arange(L, dtype=jnp.float32)[:, None] / L
    f = jnp.linspace(1e-4, bands - 1, bands, dtype=jnp.float32)[None, :]
    z = jnp.concatenate([t, jnp.cos(f * w), -jnp.sin(f * w)], axis=-1)
    fr = freq.astype(jnp.float32)
    hdn = jnp.sin(fr * (z @ w1.astype(jnp.float32) + b1.astype(jnp.float32)))
    hdn = jnp.sin(fr * (hdn @ w2.astype(jnp.float32) + b2.astype(jnp.float32)))
    hdn = jnp.sin(fr * (hdn @ w3.astype(jnp.float32) + b3.astype(jnp.float32)))
    filt = (hdn @ w4.astype(jnp.float32)).reshape(L, 2, HYENA_WIDTH)
    min_decay = math.log(DECAY_TARGET) / DECAY_FAST_PCT
    max_decay = math.log(DECAY_TARGET) / DECAY_SLOW_PCT
    deltas = jnp.linspace(min_decay, max_decay, HYENA_WIDTH, dtype=jnp.float32)
    decay = jnp.exp(-t[:, :, None] * jnp.abs(deltas))
    return filt * decay


def bidirectional_long_conv(u, filt):
    B, L, C = u.shape
    n_fft = 2 * L
    k = jnp.concatenate([filt[:, 0], jnp.zeros((1, C), jnp.float32), filt[:0:-1, 1]], axis=0)
    u_f = jnp.fft.rfft(u.astype(jnp.float32), n=n_fft, axis=1)
    k_f = jnp.fft.rfft(k, n=n_fft, axis=0)
    return jnp.fft.irfft(u_f * k_f[None], n=n_fft, axis=1)[:, :L]


def encoder_layer(x, norm_gain, w_in, conv_w, conv_b, filt_w1, filt_b1, filt_w2, filt_b2,
                  filt_w3, filt_b3, filt_w4, filt_freq, hyena_skip, w_branch_attn,
                  w_branch_hyena, w_out):
    B, L, _ = x.shape
    h = rms_norm(x, norm_gain)
    proj = jnp.einsum('bld,dc->blc', h, w_in)
    q, k, v, z_attn, u_hy, z_hy, g_attn, g_hy = jnp.split(proj, SPLIT_POINTS, axis=-1)

    cos, sin = rope_tables(L)
    shape5 = (B, L, N_GROUPS, N_SLOTS, HEAD_DIM)
    q = apply_partial_rope(q.reshape(shape5), cos, sin)
    k = apply_partial_rope(k.reshape(shape5), cos, sin)
    v = v.reshape(shape5)
    outs, lses = [], []
    for g, (window, dilation) in enumerate(ATTN_GROUPS):
        o, lse = dilated_window_attention(q[:, :, g], k[:, :, g], v[:, :, g], window, dilation)
        outs.append(o)
        lses.append(lse)
    alpha = jax.nn.softmax(jnp.stack(lses, axis=0), axis=0)
    attn = jnp.einsum('gblh,gblhe->blhe', alpha, jnp.stack(outs, axis=0))
    attn = attn.reshape(B, L, ATTN_WIDTH).astype(x.dtype) * jax.nn.silu(z_attn)

    u = short_conv(u_hy, conv_w, conv_b)
    x0, x1, vh = jnp.split(u, 3, axis=-1)
    gated = (vh * x1).astype(jnp.float32)
    filt = implicit_filters(L, filt_w1, filt_b1, filt_w2, filt_b2, filt_w3, filt_b3, filt_w4, filt_freq)
    y = bidirectional_long_conv(gated, filt) + gated * hyena_skip.astype(jnp.float32)
    hyena = (y * x0.astype(jnp.float32)).astype(x.dtype) * jax.nn.silu(z_hy)

    branch_a = jnp.einsum('blc,cd->bld', attn, w_branch_attn)
    branch_h = jnp.einsum('blc,cd->bld', hyena, w_branch_hyena)
    merged = jax.nn.sigmoid(g_attn) * branch_a + jax.nn.sigmoid(g_hy) * branch_h
    return x + jnp.einsum('bld,de->ble', merged, w_out)


def setup_inputs(seed: int = 0) -> dict:
    key = jax.random.key(seed)
    ks = jax.random.split(key, 20)
    nrm = jax.random.normal
    f32 = jnp.float32
    return {
        "x_prompt": nrm(ks[0], (BATCH, SEQ, D_MODEL), f32),
        "x_sample": nrm(ks[1], (DEC_BATCH, DEC_SEQ, D_MODEL), f32),
        "norm_gain": 1.0 + 0.01 * nrm(ks[2], (DEPTH, D_MODEL), f32),
        "w_in": nrm(ks[3], (DEPTH, D_MODEL, IN_WIDTH), f32) * D_MODEL ** -0.5,
        "conv_w": nrm(ks[4], (DEPTH, SHORT_CONV, 3 * HYENA_WIDTH), f32) * SHORT_CONV ** -0.5,
        "conv_b": 0.02 * nrm(ks[5], (DEPTH, 3 * HYENA_WIDTH), f32),
        "filt_w1": nrm(ks[6], (DEPTH, FILTER_EMB, FILTER_HIDDEN), f32) * FILTER_EMB ** -0.5,
        "filt_b1": 0.1 * nrm(ks[7], (DEPTH, FILTER_HIDDEN), f32),
        "filt_w2": nrm(ks[8], (DEPTH, FILTER_HIDDEN, FILTER_HIDDEN), f32) * FILTER_HIDDEN ** -0.5,
        "filt_b2": 0.1 * nrm(ks[9], (DEPTH, FILTER_HIDDEN), f32),
        "filt_w3": nrm(ks[10], (DEPTH, FILTER_HIDDEN, FILTER_HIDDEN), f32) * FILTER_HIDDEN ** -0.5,
        "filt_b3": 0.1 * nrm(ks[11], (DEPTH, FILTER_HIDDEN), f32),
        "filt_w4": nrm(ks[12], (DEPTH, FILTER_HIDDEN, 2 * HYENA_WIDTH), f32) * (FILTER_OUT_SCALE * FILTER_HIDDEN ** -0.5),
        "filt_freq": 1.0 + 0.01 * nrm(ks[13], (DEPTH, FILTER_HIDDEN), f32),
        "hyena_skip": 0.5 * nrm(ks[14], (DEPTH, HYENA_WIDTH), f32),
        "w_branch_attn": nrm(ks[15], (DEPTH, ATTN_WIDTH, D_MODEL), f32) * ATTN_WIDTH ** -0.5,
        "w_branch_hyena": nrm(ks[16], (DEPTH, HYENA_WIDTH, D_MODEL), f32) * HYENA_WIDTH ** -0.5,
        "w_out": nrm(ks[17], (DEPTH, D_MODEL, D_MODEL), f32) * D_MODEL ** -0.5,
        "final_gain": 1.0 + 0.01 * nrm(ks[18], (D_MODEL,), f32),
    }


def reference(x_prompt, x_sample, norm_gain, w_in, conv_w, conv_b, filt_w1, filt_b1, filt_w2,
              filt_b2, filt_w3, filt_b3, filt_w4, filt_freq, hyena_skip, w_branch_attn,
              w_branch_hyena, w_out, final_gain):
    def trunk(x):
        for i in range(DEPTH):
            x = encoder_layer(x, norm_gain[i], w_in[i], conv_w[i], conv_b[i], filt_w1[i], filt_b1[i],
                              filt_w2[i], filt_b2[i], filt_w3[i], filt_b3[i], filt_w4[i], filt_freq[i],
                              hyena_skip[i], w_branch_attn[i], w_branch_hyena[i], w_out[i])
        return rms_norm(x, final_gain)

    y_prompt = trunk(x_prompt)
    y_sample = trunk(x_sample)
    return (y_prompt, y_sample)
```

```python
import functools
import math

import jax
import jax.numpy as jnp
import numpy as np
from jax import lax
from jax.experimental import pallas as pl
from jax.experimental.pallas import tpu as pltpu

HEAD_DIM = 128
N_SLOTS = 8
ATTN_GROUPS = ((128, 1), (512, 4), (2048, 16))
N_GROUPS = len(ATTN_GROUPS)
ATTN_WIDTH = N_SLOTS * HEAD_DIM
QKV_WIDTH = N_GROUPS * ATTN_WIDTH
ROT_DIM = HEAD_DIM // 4
ROPE_THETA = 500000.0
HYENA_WIDTH = 1024
FILTER_EMB = 33
FILTER_HIDDEN = 64
DECAY_TARGET = 1e-2
DECAY_FAST_PCT = 0.3
DECAY_SLOW_PCT = 1.5
RMS_EPS = 1e-6
RADIUS = 64

LANES = 128
SUBLANES = 8
VMEM_LIMIT = 56 * 1024 * 1024

DFT_N1 = 512
DFT_HALF = DFT_N1 // 2
DFT_ROWS = 264
DFT_M = 2 * DFT_ROWS

PROJ_TM = 512
ATTN_TQ = 512
ATTN_SUB = 128
HY_TL = 512
MERGE_TM = 512
FILT_TR = 512


def _cparams(n_axes):
    return pltpu.CompilerParams(dimension_semantics=("parallel",) * n_axes,
                                vmem_limit_bytes=VMEM_LIMIT)


def _sigmoid(x):
    return 1.0 / (1.0 + jnp.exp(-x))


def _split_bf16(x):
    hi = x.astype(jnp.bfloat16)
    lo = (x - hi.astype(jnp.float32)).astype(jnp.bfloat16)
    return hi, lo


def _normed(x_ref, g_ref):
    x = x_ref[...]
    ms = jnp.mean(x * x, axis=-1, keepdims=True)
    return (x * lax.rsqrt(ms + RMS_EPS) * g_ref[...]).astype(jnp.bfloat16)


def _proj_plain_kernel(x_ref, g_ref, w_ref, o_ref):
    h = _normed(x_ref, g_ref)
    o_ref[...] = jnp.dot(h, w_ref[...], preferred_element_type=jnp.float32).astype(o_ref.dtype)


def _proj_plain(x, gain, w_slabs, out_dtype):
    B, L, D = x.shape
    S, _, CS = w_slabs.shape
    tm = PROJ_TM
    return pl.pallas_call(
        _proj_plain_kernel,
        grid=(S, B, L // tm),
        in_specs=[pl.BlockSpec((None, tm, D), lambda s, b, i: (b, i, 0)),
                  pl.BlockSpec((1, D), lambda s, b, i: (0, 0)),
                  pl.BlockSpec((None, D, CS), lambda s, b, i: (s, 0, 0), pipeline_mode=pl.Buffered(1))],
        out_specs=pl.BlockSpec((None, None, tm, CS), lambda s, b, i: (s, b, i, 0)),
        out_shape=jax.ShapeDtypeStruct((S, B, L, CS), out_dtype),
        compiler_params=_cparams(3),
        name="proj_plain",
    )(x, gain, w_slabs)


def _proj_qkv_kernel(x_ref, g_ref, w_ref, cq_ref, sq_ref, ck_ref, sk_ref, o_ref, scr, *, dil):
    tm = x_ref.shape[0]
    h = _normed(x_ref, g_ref)
    acc = jnp.dot(h, w_ref[...], preferred_element_type=jnp.float32)
    lane = lax.broadcasted_iota(jnp.int32, (tm, HEAD_DIM), 1)
    first_half = lane < (ROT_DIM // 2)
    n_tiles = 3 * N_SLOTS
    for c in range(n_tiles):
        t = acc[:, c * HEAD_DIM:(c + 1) * HEAD_DIM]
        which = c // N_SLOTS
        if which < 2:
            cos_ref, sin_ref = (cq_ref, sq_ref) if which == 0 else (ck_ref, sk_ref)
            partner = jnp.where(first_half,
                                pltpu.roll(t, HEAD_DIM - ROT_DIM // 2, axis=1),
                                pltpu.roll(t, ROT_DIM // 2, axis=1))
            t = t * cos_ref[...] + partner * sin_ref[...]
        if dil == 1:
            o_ref[which, 0, :, (c % N_SLOTS) * HEAD_DIM:(c % N_SLOTS + 1) * HEAD_DIM] = t.astype(o_ref.dtype)
        else:
            scr[c] = t
    if dil > 1:
        rows = tm // dil
        for c in range(n_tiles):
            which, slot = c // N_SLOTS, c % N_SLOTS
            for r in range(dil):
                o_ref[which, r, :, slot * HEAD_DIM:(slot + 1) * HEAD_DIM] = (
                    scr[c, pl.ds(r, rows, stride=dil), :].astype(o_ref.dtype))


def _proj_qkv(x, gain, w_slab, tabs, dil):
    B, L, D = x.shape
    CS = w_slab.shape[-1]
    tm = PROJ_TM
    n = L // dil
    tab_spec = pl.BlockSpec((tm, HEAD_DIM), lambda b, i: (i, 0))
    return pl.pallas_call(
        functools.partial(_proj_qkv_kernel, dil=dil),
        grid=(B, L // tm),
        in_specs=[pl.BlockSpec((None, tm, D), lambda b, i: (b, i, 0)),
                  pl.BlockSpec((1, D), lambda b, i: (0, 0)),
                  pl.BlockSpec((D, CS), lambda b, i: (0, 0), pipeline_mode=pl.Buffered(1)),
                  tab_spec, tab_spec, tab_spec, tab_spec],
        out_specs=pl.BlockSpec((3, None, dil, tm // dil, ATTN_WIDTH), lambda b, i: (0, b, 0, i, 0)),
        out_shape=jax.ShapeDtypeStruct((3, B, dil, n, ATTN_WIDTH), jnp.bfloat16),
        scratch_shapes=[pltpu.VMEM((3 * N_SLOTS, tm, HEAD_DIM), jnp.float32)],
        compiler_params=_cparams(2),
        name=f"proj_qkv_d{dil}",
    )(x, gain, w_slab, *tabs)


def _rope_tables(L):
    inv_freq = jnp.power(ROPE_THETA, -jnp.arange(0, ROT_DIM, 2, dtype=jnp.float32) / ROT_DIM)
    ang = jnp.arange(L, dtype=jnp.float32)[:, None] * inv_freq[None, :]
    cos, sin = jnp.cos(ang), jnp.sin(ang)
    ones = jnp.ones((L, HEAD_DIM - ROT_DIM), jnp.float32)
    cos_t = jnp.concatenate([cos, cos, ones], axis=1)
    sin_t = jnp.concatenate([-sin, sin, 0.0 * ones], axis=1)
    scale = 1.0 / math.sqrt(HEAD_DIM)
    return cos_t * scale, sin_t * scale, cos_t, sin_t


def _attn_kernel(q_ref, kp_ref, kc_ref, kn_ref, vp_ref, vc_ref, vn_ref, o_ref, lse_ref, kw, vw, *, n):
    tq = q_ref.shape[0]
    i = pl.program_id(1)
    kw[0:RADIUS] = kp_ref[...]
    kw[RADIUS:RADIUS + tq] = kc_ref[...]
    kw[RADIUS + tq:] = kn_ref[...]
    vw[0:RADIUS] = vp_ref[...]
    vw[RADIUS:RADIUS + tq] = vc_ref[...]
    vw[RADIUS + tq:] = vn_ref[...]
    win = ATTN_SUB + 2 * RADIUS
    row = lax.broadcasted_iota(jnp.int32, (ATTN_SUB, win), 0)
    col = lax.broadcasted_iota(jnp.int32, (ATTN_SUB, win), 1)
    band = (col >= row) & (col <= row + 2 * RADIUS)
    lane = lax.broadcasted_iota(jnp.int32, (ATTN_SUB, LANES), 1)
    nsub = tq // ATTN_SUB
    for s in range(nsub):
        q0 = i * tq + s * ATTN_SUB
        valid = band
        if s == 0:
            valid = valid & (col >= RADIUS - q0)
        if s == nsub - 1:
            valid = valid & (col < n - q0 + RADIUS)
        lse_tile = jnp.zeros((ATTN_SUB, LANES), jnp.float32)
        for h in range(N_SLOTS):
            cs = slice(h * HEAD_DIM, (h + 1) * HEAD_DIM)
            qb = q_ref[s * ATTN_SUB:(s + 1) * ATTN_SUB, cs]
            kb = kw[s * ATTN_SUB:s * ATTN_SUB + win, cs]
            vb = vw[s * ATTN_SUB:s * ATTN_SUB + win, cs]
            sc = lax.dot_general(qb, kb, (((1,), (1,)), ((), ())), preferred_element_type=jnp.float32)
            sc = jnp.where(valid, sc, -1e30)
            m = jnp.max(sc, axis=1, keepdims=True)
            p = jnp.exp(sc - m)
            l = jnp.sum(p, axis=1, keepdims=True)
            o = jnp.dot(p.astype(jnp.bfloat16), vb, preferred_element_type=jnp.float32)
            o_ref[s * ATTN_SUB:(s + 1) * ATTN_SUB, cs] = o / l
            lse_tile = jnp.where(lane == h, m + jnp.log(l), lse_tile)
        lse_ref[s * ATTN_SUB:(s + 1) * ATTN_SUB, :] = lse_tile


def _attention(qkv):
    _, S, n, W = qkv.shape
    tq = min(ATTN_TQ, n)
    r = tq // RADIUS
    last = n // RADIUS - 1

    def cur(which):
        return pl.BlockSpec((None, None, tq, W), lambda s, i: (which, s, i, 0))

    def prev(which):
        return pl.BlockSpec((None, None, RADIUS, W), lambda s, i: (which, s, jnp.maximum(i * r - 1, 0), 0))

    def nxt(which):
        return pl.BlockSpec((None, None, RADIUS, W), lambda s, i: (which, s, jnp.minimum((i + 1) * r, last), 0))

    return pl.pallas_call(
        functools.partial(_attn_kernel, n=n),
        grid=(S, n // tq),
        in_specs=[cur(0), prev(1), cur(1), nxt(1), prev(2), cur(2), nxt(2)],
        out_specs=[pl.BlockSpec((None, tq, W), lambda s, i: (s, i, 0)),
                   pl.BlockSpec((None, tq, LANES), lambda s, i: (s, i, 0))],
        out_shape=[jax.ShapeDtypeStruct((S, n, W), jnp.float32),
                   jax.ShapeDtypeStruct((S, n, LANES), jnp.float32)],
        scratch_shapes=[pltpu.VMEM((tq + 2 * RADIUS, W), jnp.bfloat16),
                        pltpu.VMEM((tq + 2 * RADIUS, W), jnp.bfloat16)],
        compiler_params=_cparams(2),
        name="attn",
    )(qkv, qkv, qkv, qkv, qkv, qkv, qkv)


def _hyena_pre_kernel(u_ref, up_ref, un_ref, z_ref, cw_ref, cb_ref, g_ref, m_ref, scr_g, scr_m, *, n2, nblk):
    tl = u_ref.shape[0]
    i = pl.program_id(1)
    u = u_ref[...]
    has_prev = (i > 0).astype(jnp.float32)
    has_next = (i < nblk - 1).astype(jnp.float32)
    prev_row = up_ref[SUBLANES - 1:SUBLANES, :] * has_prev
    next_row = un_ref[0:1, :] * has_next
    ridx = lax.broadcasted_iota(jnp.int32, u.shape, 0)
    u_prev = jnp.where(ridx == 0, prev_row, pltpu.roll(u, 1, axis=0))
    u_next = jnp.where(ridx == tl - 1, next_row, pltpu.roll(u, tl - 1, axis=0))
    conv = u_prev * cw_ref[0:1, :] + u * cw_ref[1:2, :] + u_next * cw_ref[2:3, :] + cb_ref[...]
    C = HYENA_WIDTH
    x0, x1, vh = conv[:, :C], conv[:, C:2 * C], conv[:, 2 * C:]
    gated = vh * x1
    z = z_ref[...].astype(jnp.float32)
    mult = x0 * (z * _sigmoid(z))
    rows = tl // n2
    for c in range(C // LANES):
        scr_g[c] = gated[:, c * LANES:(c + 1) * LANES]
        scr_m[c] = mult[:, c * LANES:(c + 1) * LANES]
    for c in range(C // LANES):
        for r in range(n2):
            g_ref[r, :, c * LANES:(c + 1) * LANES] = scr_g[c, pl.ds(r, rows, stride=n2), :]
            m_ref[r, :, c * LANES:(c + 1) * LANES] = scr_m[c, pl.ds(r, rows, stride=n2), :]


def _hyena_pre(u_hy, gates, conv_w, conv_b, n2):
    B, L, C3 = u_hy.shape
    C = HYENA_WIDTH
    tl = HY_TL
    nblk = L // tl
    per = tl // SUBLANES
    last8 = L // SUBLANES - 1
    out_spec = pl.BlockSpec((None, n2, tl // n2, C), lambda b, i: (b, 0, i, 0))
    out_sds = jax.ShapeDtypeStruct((B, n2, L // n2, C), jnp.float32)
    return pl.pallas_call(
        functools.partial(_hyena_pre_kernel, n2=n2, nblk=nblk),
        grid=(B, nblk),
        in_specs=[pl.BlockSpec((None, tl, C3), lambda b, i: (b, i, 0)),
                  pl.BlockSpec((None, SUBLANES, C3), lambda b, i: (b, jnp.maximum(i * per - 1, 0), 0)),
                  pl.BlockSpec((None, SUBLANES, C3), lambda b, i: (b, jnp.minimum((i + 1) * per, last8), 0)),
                  pl.BlockSpec((None, None, tl, C), lambda b, i: (0, b, i, 1)),
                  pl.BlockSpec((3, C3), lambda b, i: (0, 0)),
                  pl.BlockSpec((1, C3), lambda b, i: (0, 0))],
        out_specs=[out_spec, out_spec],
        out_shape=[out_sds, out_sds],
        scratch_shapes=[pltpu.VMEM((C // LANES, tl, LANES), jnp.float32),
                        pltpu.VMEM((C // LANES, tl, LANES), jnp.float32)],
        compiler_params=_cparams(2),
        name="hyena_pre",
    )(u_hy, u_hy, u_hy, gates, conv_w, conv_b)


POS_DIR, POS_VALID, POS_ZERO = FILTER_EMB, FILTER_EMB + 1, FILTER_EMB + 2


def _filter_kernel(p_ref, w1_ref, b1_ref, w2_ref, b2_ref, w3_ref, b3_ref, w4_ref, fr_ref, dl_ref, sk_ref, o_ref):
    hp = lax.Precision.HIGHEST
    p = p_ref[...]
    fr = fr_ref[...]
    hdn = jnp.sin(fr * (jnp.dot(p, w1_ref[...], precision=hp, preferred_element_type=jnp.float32) + b1_ref[...]))
    hdn = jnp.sin(fr * (jnp.dot(hdn, w2_ref[...], precision=hp, preferred_element_type=jnp.float32) + b2_ref[...]))
    hdn = jnp.sin(fr * (jnp.dot(hdn, w3_ref[...], precision=hp, preferred_element_type=jnp.float32) + b3_ref[...]))
    filt = jnp.dot(hdn, w4_ref[...], precision=hp, preferred_element_type=jnp.float32)
    C = HYENA_WIDTH
    t = p[:, 0:1]
    backward = p[:, POS_DIR:POS_DIR + 1] > 0.5
    valid = p[:, POS_VALID:POS_VALID + 1]
    at_zero = p[:, POS_ZERO:POS_ZERO + 1]
    val = jnp.where(backward, filt[:, C:], filt[:, :C])
    decay = jnp.exp(-t * dl_ref[...])
    o_ref[...] = val * decay * valid + at_zero * sk_ref[...]


def _filter_lag_table(L, n2):
    bands = (FILTER_EMB - 1) // 2
    t = jnp.linspace(0.0, 1.0, L, dtype=jnp.float32)[:, None]
    w = 2.0 * math.pi * jnp.arange(L, dtype=jnp.float32)[:, None] / L
    f = jnp.linspace(1e-4, bands - 1, bands, dtype=jnp.float32)[None, :]
    z = jnp.concatenate([t, jnp.cos(f * w), -jnp.sin(f * w)], axis=-1)
    n = jnp.arange(2 * L, dtype=jnp.int32)
    j = jnp.where(n < L, n, 2 * L - n)
    zj = jnp.take(z, jnp.minimum(j, L - 1), axis=0)
    flags = jnp.stack([(n > L), (n != L), (n == 0)], axis=1).astype(jnp.float32)
    tab = jnp.concatenate([zj, flags, jnp.zeros((2 * L, LANES - FILTER_EMB - 3), jnp.float32)], axis=1)
    return tab.reshape(2 * L // n2, n2, LANES).transpose(1, 0, 2).reshape(2 * L, LANES)


def _filter(L, n2, lp):
    C = HYENA_WIDTH
    tab = _filter_lag_table(L, n2)
    w1p = jnp.zeros((LANES, FILTER_HIDDEN), jnp.float32).at[:FILTER_EMB].set(lp["filt_w1"])
    min_decay = math.log(DECAY_TARGET) / DECAY_FAST_PCT
    max_decay = math.log(DECAY_TARGET) / DECAY_SLOW_PCT
    deltas = jnp.abs(jnp.linspace(min_decay, max_decay, C, dtype=jnp.float32))[None, :]
    tr = FILT_TR
    full = lambda a: pl.BlockSpec(a.shape, lambda i: (0,) * a.ndim)
    args = [w1p, lp["filt_b1"][None, :], lp["filt_w2"], lp["filt_b2"][None, :], lp["filt_w3"],
            lp["filt_b3"][None, :], lp["filt_w4"], lp["filt_freq"][None, :], deltas, lp["hyena_skip"][None, :]]
    k = pl.pallas_call(
        _filter_kernel,
        grid=(2 * L // tr,),
        in_specs=[pl.BlockSpec((tr, LANES), lambda i: (i, 0))] + [full(a) for a in args],
        out_specs=pl.BlockSpec((tr, C), lambda i: (i, 0)),
        out_shape=jax.ShapeDtypeStruct((2 * L, C), jnp.float32),
        compiler_params=_cparams(1),
        name="filter",
    )(tab, *args)
    return k.reshape(1, n2, DFT_N1, C)


def _dft_tables(n2):
    N = DFT_N1 * n2
    r = jnp.arange(n2, dtype=jnp.int32)[:, None, None]
    row = jnp.arange(DFT_M, dtype=jnp.int32)[None, :, None]
    j = jnp.arange(DFT_N1, dtype=jnp.int32)[None, None, :]
    is_im = row >= DFT_ROWS
    k1 = jnp.where(is_im, row - DFT_ROWS, row)
    live = k1 <= DFT_HALF
    phase = (k1 * (j * n2 + r)) % N
    ang = phase.astype(jnp.float32) * (2.0 * math.pi / N)
    F = jnp.where(live, jnp.where(is_im, -jnp.sin(ang), jnp.cos(ang)), 0.0)
    weight = jnp.where((k1 == 0) | (k1 == DFT_HALF), 1.0, 2.0) / N
    G = jnp.swapaxes(F[:, :, :DFT_HALF] * weight, 1, 2)
    return _split_bf16(F), _split_bf16(G)


def _dft_mm_kernel(x_ref, mh_ref, ml_ref, *rest, n_slabs):
    o_ref = rest[-1]
    g_ref = rest[0] if len(rest) == 2 else None
    for s in range(n_slabs):
        xh, xl = _split_bf16(x_ref[s])
        mh = mh_ref[s]
        y = (jnp.dot(mh, xh, preferred_element_type=jnp.float32)
             + jnp.dot(ml_ref[s], xh, preferred_element_type=jnp.float32)
             + jnp.dot(mh, xl, preferred_element_type=jnp.float32))
        if g_ref is not None:
            y = y * g_ref[s]
        o_ref[s] = y


def _dft_mm(x, mats, k_used, gate=None, tc=256, slabs=4):
    mh, ml = mats
    Bx, n2, K, C = x.shape
    assert K == k_used
    M = mh.shape[1]
    slabs = min(slabs, n2)
    x_spec = pl.BlockSpec((None, slabs, K, tc), lambda c, b, r: (b, r, 0, c))
    m_spec = pl.BlockSpec((slabs, M, K), lambda c, b, r: (r, 0, 0))
    o_spec = pl.BlockSpec((None, slabs, M, tc), lambda c, b, r: (b, r, 0, c))
    in_specs = [x_spec, m_spec, m_spec]
    args = [x, mh, ml]
    if gate is not None:
        in_specs.append(o_spec)
        args.append(gate)
    return pl.pallas_call(
        functools.partial(_dft_mm_kernel, n_slabs=slabs),
        grid=(C // tc, Bx, n2 // slabs),
        in_specs=in_specs,
        out_specs=o_spec,
        out_shape=jax.ShapeDtypeStruct((Bx, n2, M, C), jnp.float32),
        compiler_params=_cparams(3),
        name="dft_mm",
    )(*args)


def _fft_list(xs, sign):
    n = len(xs)
    if n == 1:
        return xs
    even = _fft_list(xs[0::2], sign)
    odd = _fft_list(xs[1::2], sign)
    out = [None] * n
    for k in range(n // 2):
        er, ei = even[k]
        pr, pi = odd[k]
        if k == 0:
            tr, ti = pr, pi
        elif 4 * k == n:
            tr, ti = (pi, -pr) if sign < 0 else (-pi, pr)
        else:
            ang = sign * 2.0 * math.pi * k / n
            wr, wi = math.cos(ang), math.sin(ang)
            tr, ti = pr * wr - pi * wi, pr * wi + pi * wr
        out[k] = (er + tr, ei + ti)
        out[k + n // 2] = (er - tr, ei - ti)
    return out


def _dft_mid_kernel(s_ref, *rest, n2, with_filter):
    o_ref = rest[-1]
    kf_ref = rest[0] if with_filter else None
    rows = s_ref.shape[2]

    def body(c, carry):
        rs = pl.ds(pl.multiple_of(c * SUBLANES, SUBLANES), SUBLANES)
        xs = [(s_ref[r, 0, rs, :], s_ref[r, 1, rs, :]) for r in range(n2)]
        ys = _fft_list(xs, -1)
        if with_filter:
            prod = []
            for k2 in range(n2):
                yr, yi = ys[k2]
                fr, fi = kf_ref[k2, 0, rs, :], kf_ref[k2, 1, rs, :]
                prod.append((yr * fr - yi * fi, yr * fi + yi * fr))
            ys = _fft_list(prod, +1)
        for r in range(n2):
            o_ref[r, 0, rs, :] = ys[r][0]
            o_ref[r, 1, rs, :] = ys[r][1]
        return carry

    lax.fori_loop(0, rows // SUBLANES, body, 0)


def _dft_mid(s, kf=None, tc=128, tr=88):
    Bx, n2, M, C = s.shape
    s5 = s.reshape(Bx, n2, 2, DFT_ROWS, C)
    blk = pl.BlockSpec((None, n2, 2, tr, tc), lambda c, r, b: (b, 0, 0, r, c))
    in_specs, args = [blk], [s5]
    if kf is not None:
        in_specs.append(pl.BlockSpec((n2, 2, tr, tc), lambda c, r, b: (0, 0, r, c)))
        args.append(kf.reshape(n2, 2, DFT_ROWS, C))
    out = pl.pallas_call(
        functools.partial(_dft_mid_kernel, n2=n2, with_filter=kf is not None),
        grid=(C // tc, DFT_ROWS // tr, Bx),
        in_specs=in_specs,
        out_specs=blk,
        out_shape=jax.ShapeDtypeStruct((Bx, n2, 2, DFT_ROWS, C), jnp.float32),
        compiler_params=_cparams(3),
        name="dft_mid",
    )(*args)
    return out.reshape(Bx, n2, M, C)


def _regroup_to_tokens(blk_ref, scr, dil, width):
    m = blk_ref.shape[1]
    for c in range(width // LANES):
        for r in range(dil):
            scr[c, pl.ds(r, m, stride=dil), :] = blk_ref[r, :, c * LANES:(c + 1) * LANES]


def _merge_kernel(o0_ref, l0_ref, o1_ref, l1_ref, o2_ref, l2_ref, hy_ref, za_ref, ga_ref, gh_ref,
                  wa_ref, wh_ref, out_ref, so1, so2, sl1, sl2, shy, attn_scr, hy_scr, *, n2):
    d1, d2 = ATTN_GROUPS[1][1], ATTN_GROUPS[2][1]
    _regroup_to_tokens(o1_ref, so1, d1, ATTN_WIDTH)
    _regroup_to_tokens(o2_ref, so2, d2, ATTN_WIDTH)
    _regroup_to_tokens(l1_ref, sl1, d1, LANES)
    _regroup_to_tokens(l2_ref, sl2, d2, LANES)
    _regroup_to_tokens(hy_ref, shy, n2, HYENA_WIDTH)
    l0, l1, l2 = l0_ref[...], sl1[0], sl2[0]
    mx = jnp.maximum(jnp.maximum(l0, l1), l2)
    e0, e1, e2 = jnp.exp(l0 - mx), jnp.exp(l1 - mx), jnp.exp(l2 - mx)
    inv = 1.0 / (e0 + e1 + e2)
    a0, a1, a2 = e0 * inv, e1 * inv, e2 * inv
    for h in range(N_SLOTS):
        cs = slice(h * HEAD_DIM, (h + 1) * HEAD_DIM)
        mix = a0[:, h:h + 1] * o0_ref[:, cs] + a1[:, h:h + 1] * so1[h] + a2[:, h:h + 1] * so2[h]
        z = za_ref[:, cs].astype(jnp.float32)
        attn_scr[:, cs] = (mix * (z * _sigmoid(z))).astype(jnp.bfloat16)
        hy_scr[:, cs] = shy[h].astype(jnp.bfloat16)
    br_a = jnp.dot(attn_scr[...], wa_ref[...], preferred_element_type=jnp.float32)
    br_h = jnp.dot(hy_scr[...], wh_ref[...], preferred_element_type=jnp.float32)
    merged = (_sigmoid(ga_ref[...].astype(jnp.float32)) * br_a
              + _sigmoid(gh_ref[...].astype(jnp.float32)) * br_h)
    out_ref[...] = merged.astype(out_ref.dtype)


def _merge(outs, lses, hyena, gates, w_ba, w_bh, n2):
    B, d0, L, W = outs[0].shape
    D = w_ba.shape[1]
    tm = MERGE_TM
    d1, d2 = ATTN_GROUPS[1][1], ATTN_GROUPS[2][1]

    def cls(d, width):
        return pl.BlockSpec((None, d, tm // d, width), lambda b, i: (b, 0, i, 0))

    in_specs = [pl.BlockSpec((None, None, tm, W), lambda b, i: (b, 0, i, 0)),
                pl.BlockSpec((None, None, tm, LANES), lambda b, i: (b, 0, i, 0)),
                cls(d1, W), cls(d1, LANES), cls(d2, W), cls(d2, LANES), cls(n2, HYENA_WIDTH),
                pl.BlockSpec((None, None, tm, W), lambda b, i: (0, b, i, 0)),
                pl.BlockSpec((None, None, tm, D), lambda b, i: (1, b, i, 0)),
                pl.BlockSpec((None, None, tm, D), lambda b, i: (2, b, i, 0)),
                pl.BlockSpec(w_ba.shape, lambda b, i: (0, 0), pipeline_mode=pl.Buffered(1)),
                pl.BlockSpec(w_bh.shape, lambda b, i: (0, 0), pipeline_mode=pl.Buffered(1))]
    wt = W // LANES
    return pl.pallas_call(
        functools.partial(_merge_kernel, n2=n2),
        grid=(B, L // tm),
        in_specs=in_specs,
        out_specs=pl.BlockSpec((None, tm, D), lambda b, i: (b, i, 0)),
        out_shape=jax.ShapeDtypeStruct((B, L, D), jnp.bfloat16),
        scratch_shapes=[pltpu.VMEM((wt, tm, LANES), jnp.float32), pltpu.VMEM((wt, tm, LANES), jnp.float32),
                        pltpu.VMEM((1, tm, LANES), jnp.float32), pltpu.VMEM((1, tm, LANES), jnp.float32),
                        pltpu.VMEM((wt, tm, LANES), jnp.float32),
                        pltpu.VMEM((tm, W), jnp.bfloat16), pltpu.VMEM((tm, HYENA_WIDTH), jnp.bfloat16)],
        compiler_params=_cparams(2),
        name="merge",
    )(outs[0], lses[0], outs[1], lses[1], outs[2], lses[2], hyena, gates, gates, gates, w_ba, w_bh)


def _out_kernel(m_ref, w_ref, x_ref, g_ref, o_ref, *, final_norm):
    y = x_ref[...] + jnp.dot(m_ref[...], w_ref[...], preferred_element_type=jnp.float32)
    if final_norm:
        ms = jnp.mean(y * y, axis=-1, keepdims=True)
        y = y * lax.rsqrt(ms + RMS_EPS) * g_ref[...]
    o_ref[...] = y


def _out_proj(merged, w_out, x, final_gain, final_norm):
    B, L, D = x.shape
    tm = MERGE_TM
    tok = pl.BlockSpec((None, tm, D), lambda b, i: (b, i, 0))
    return pl.pallas_call(
        functools.partial(_out_kernel, final_norm=final_norm),
        grid=(B, L // tm),
        in_specs=[tok, pl.BlockSpec(w_out.shape, lambda b, i: (0, 0), pipeline_mode=pl.Buffered(1)),
                  tok, pl.BlockSpec((1, D), lambda b, i: (0, 0))],
        out_specs=tok,
        out_shape=jax.ShapeDtypeStruct((B, L, D), jnp.float32),
        compiler_params=_cparams(2),
        name="out_proj",
    )(merged, w_out, x, final_gain)


def _prep_layer(i, norm_gain, w_in, conv_w, conv_b, filt_w1, filt_b1, filt_w2, filt_b2, filt_w3, filt_b3,
                filt_w4, filt_freq, hyena_skip, w_branch_attn, w_branch_hyena, w_out):
    bf = jnp.bfloat16
    w = w_in[i]
    D = w.shape[0]
    q, k, v = (w[:, t * QKV_WIDTH:(t + 1) * QKV_WIDTH].reshape(D, N_GROUPS, ATTN_WIDTH) for t in range(3))
    o = 3 * QKV_WIDTH
    z_attn = w[:, o:o + ATTN_WIDTH]
    u_hy = w[:, o + ATTN_WIDTH:o + ATTN_WIDTH + 3 * HYENA_WIDTH]
    o2 = o + ATTN_WIDTH + 3 * HYENA_WIDTH
    z_hy = w[:, o2:o2 + HYENA_WIDTH]
    g_attn = w[:, o2 + HYENA_WIDTH:o2 + HYENA_WIDTH + D]
    g_hy = w[:, o2 + HYENA_WIDTH + D:]
    return dict(
        gain=norm_gain[i][None, :],
        w_qkv=[jnp.concatenate([q[:, g], k[:, g], v[:, g]], axis=1).astype(bf) for g in range(N_GROUPS)],
        w_uhy=u_hy.astype(bf)[None],
        w_gates=jnp.stack([jnp.concatenate([z_attn, z_hy], axis=1), g_attn, g_hy]).astype(bf),
        conv_w=conv_w[i], conv_b=conv_b[i][None, :],
        filt_w1=filt_w1[i], filt_b1=filt_b1[i], filt_w2=filt_w2[i], filt_b2=filt_b2[i],
        filt_w3=filt_w3[i], filt_b3=filt_b3[i], filt_w4=filt_w4[i], filt_freq=filt_freq[i],
        hyena_skip=hyena_skip[i],
        w_ba=w_branch_attn[i].astype(bf), w_bh=w_branch_hyena[i].astype(bf), w_out=w_out[i].astype(bf))


def _layer(x, lp, final_gain, final_norm):
    B, L, D = x.shape
    assert L % (2 * DFT_HALF) == 0 and L % PROJ_TM == 0
    n2 = L // DFT_HALF
    C = HYENA_WIDTH

    tabs = _rope_tables(L)
    outs, lses = [], []
    for g, (_, dil) in enumerate(ATTN_GROUPS):
        qkv = _proj_qkv(x, lp["gain"], lp["w_qkv"][g], tabs, dil)
        n = L // dil
        o, lse = _attention(qkv.reshape(3, B * dil, n, ATTN_WIDTH))
        outs.append(o.reshape(B, dil, n, ATTN_WIDTH))
        lses.append(lse.reshape(B, dil, n, LANES))

    gates = _proj_plain(x, lp["gain"], lp["w_gates"], jnp.bfloat16)
    u_hy = _proj_plain(x, lp["gain"], lp["w_uhy"], jnp.float32)[0]
    gated, mult = _hyena_pre(u_hy, gates, lp["conv_w"], lp["conv_b"], n2)
    fwd_tabs, inv_tabs = _dft_tables(n2)
    kf = _dft_mid(_dft_mm(_filter(L, n2, lp), fwd_tabs, DFT_N1))[0]
    spec = _dft_mm(gated, fwd_tabs, DFT_HALF)
    spec = _dft_mid(spec, kf)
    hyena = _dft_mm(spec, inv_tabs, DFT_M, gate=mult)

    merged = _merge(outs, lses, hyena, gates, lp["w_ba"], lp["w_bh"], n2)
    return _out_proj(merged, lp["w_out"], x, final_gain, final_norm)


def kernel(x_prompt, x_sample, norm_gain, w_in, conv_w, conv_b, filt_w1, filt_b1, filt_w2, filt_b2, filt_w3, filt_b3, filt_w4, filt_freq, hyena_skip, w_branch_attn, w_branch_hyena, w_out, final_gain):
    depth = w_in.shape[0]
    layers = [_prep_layer(i, norm_gain, w_in, conv_w, conv_b, filt_w1, filt_b1, filt_w2, filt_b2, filt_w3,
                          filt_b3, filt_w4, filt_freq, hyena_skip, w_branch_attn, w_branch_hyena, w_out)
              for i in range(depth)]
    fg = final_gain[None, :]

    def trunk(x):
        for i, lp in enumerate(layers):
            x = _layer(x, lp, fg, final_norm=(i == depth - 1))
        return x

    return (trunk(x_prompt), trunk(x_sample))
```

```python
import functools
import math

import jax
import jax.numpy as jnp
import numpy as np
from jax import lax
from jax.experimental import pallas as pl
from jax.experimental.pallas import tpu as pltpu

HEAD_DIM = 128
N_SLOTS = 8
ATTN_GROUPS = ((128, 1), (512, 4), (2048, 16))
N_GROUPS = len(ATTN_GROUPS)
ATTN_WIDTH = N_SLOTS * HEAD_DIM
QKV_WIDTH = N_GROUPS * ATTN_WIDTH
ROT_DIM = HEAD_DIM // 4
ROPE_THETA = 500000.0
HYENA_WIDTH = 1024
FILTER_EMB = 33
FILTER_HIDDEN = 64
DECAY_TARGET = 1e-2
DECAY_FAST_PCT = 0.3
DECAY_SLOW_PCT = 1.5
RMS_EPS = 1e-6
RADIUS = 64

LANES = 128
SUBLANES = 8
VMEM_LIMIT = 56 * 1024 * 1024

DFT_N1 = 512
DFT_HALF = DFT_N1 // 2
DFT_ROWS = 264
DFT_M = 2 * DFT_ROWS

PROJ_TM = 512
W_BLOCK = 1024
_ZA_BLOCK = 3 * QKV_WIDTH // W_BLOCK
UHY_BLOCK = _ZA_BLOCK + ATTN_WIDTH // W_BLOCK
_ZH_BLOCK = UHY_BLOCK + 3 * HYENA_WIDTH // W_BLOCK
GATE_BLOCKS = (_ZA_BLOCK, _ZH_BLOCK, _ZH_BLOCK + HYENA_WIDTH // W_BLOCK)
PERM_MIN_DIL = 8
ATTN_TQ = 512
ATTN_SUB = 128
HY_TL = 512
MERGE_TM = 512


def _cparams(n_axes):
    return pltpu.CompilerParams(dimension_semantics=("parallel",) * n_axes,
                                vmem_limit_bytes=VMEM_LIMIT)


def _sigmoid(x):
    return 0.5 * jnp.tanh(0.5 * x) + 0.5


def _split_bf16(x):
    hi = x.astype(jnp.bfloat16)
    lo = (x - hi.astype(jnp.float32)).astype(jnp.bfloat16)
    return hi, lo


def _normed(x_ref, g_ref):
    x = x_ref[...]
    ms = jnp.mean(x * x, axis=-1, keepdims=True)
    return (x * lax.rsqrt(ms + RMS_EPS) * g_ref[...]).astype(jnp.bfloat16)


def _proj_plain_kernel(x_ref, g_ref, *rest):
    w_refs, o_ref = rest[:-1], rest[-1]
    h = _normed(x_ref, g_ref)
    for k, w_ref in enumerate(w_refs):
        cw = w_ref.shape[1]
        o_ref[:, k * cw:(k + 1) * cw] = jnp.dot(h, w_ref[...], preferred_element_type=jnp.float32).astype(o_ref.dtype)


def _proj_plain(x, gain, wb, col_maps, n_slabs, out_dtype):
    B, L, D = x.shape
    tm = PROJ_TM
    cs = len(col_maps) * W_BLOCK
    w_specs = [pl.BlockSpec((D, W_BLOCK), functools.partial(lambda s, b, i, f: (0, f(s)), f=f),
                            pipeline_mode=pl.Buffered(1)) for f in col_maps]
    return pl.pallas_call(
        _proj_plain_kernel,
        grid=(n_slabs, B, L // tm),
        in_specs=[pl.BlockSpec((None, tm, D), lambda s, b, i: (b, i, 0)),
                  pl.BlockSpec((1, D), lambda s, b, i: (0, 0))] + w_specs,
        out_specs=pl.BlockSpec((None, tm, cs), lambda s, b, i: (b, i, s)),
        out_shape=jax.ShapeDtypeStruct((B, L, n_slabs * cs), out_dtype),
        compiler_params=_cparams(3),
        name="proj_plain",
    )(x, gain, *([wb] * len(col_maps)))


def _proj_qkv_kernel(x_ref, g_ref, wq_ref, wk_ref, wv_ref, cq_ref, sq_ref, ck_ref, sk_ref, *rest, dil, perm):
    tm = x_ref.shape[0]
    rows = tm // dil
    h = _normed(x_ref, g_ref)
    if perm:
        p_ref, o_ref = rest
        h = jnp.dot(p_ref[...], h, preferred_element_type=jnp.float32).astype(jnp.bfloat16)
    elif dil > 1:
        o_ref, scr = rest
    else:
        (o_ref,) = rest
    lane = lax.broadcasted_iota(jnp.int32, (tm, HEAD_DIM), 1)
    first_half = lane < (ROT_DIM // 2)
    for which, w_ref in enumerate((wq_ref, wk_ref, wv_ref)):
        acc = jnp.dot(h, w_ref[...], preferred_element_type=jnp.float32)
        for slot in range(N_SLOTS):
            cs = slice(slot * HEAD_DIM, (slot + 1) * HEAD_DIM)
            t = acc[:, cs]
            if which < 2:
                cos_ref, sin_ref = (cq_ref, sq_ref) if which == 0 else (ck_ref, sk_ref)
                partner = jnp.where(first_half,
                                    pltpu.roll(t, HEAD_DIM - ROT_DIM // 2, axis=1),
                                    pltpu.roll(t, ROT_DIM // 2, axis=1))
                t = t * cos_ref[...] + partner * sin_ref[...]
            if perm or dil == 1:
                for r in range(dil):
                    o_ref[which, r, :, cs] = t[r * rows:(r + 1) * rows].astype(o_ref.dtype)
            else:
                scr[slot] = t
        if not perm and dil > 1:
            for slot in range(N_SLOTS):
                for r in range(dil):
                    o_ref[which, r, :, slot * HEAD_DIM:(slot + 1) * HEAD_DIM] = (
                        scr[slot, pl.ds(r, rows, stride=dil), :].astype(o_ref.dtype))


def _class_major(a, tm, dil):
    n_tiles = a.shape[0] // tm
    return a.reshape(n_tiles, tm // dil, dil, -1).transpose(0, 2, 1, 3).reshape(a.shape)


def _proj_qkv(x, gain, wb, tabs, dil, group):
    B, L, D = x.shape
    tm = PROJ_TM
    n = L // dil
    perm = dil >= PERM_MIN_DIL
    tab_spec = pl.BlockSpec((tm, HEAD_DIM), lambda b, i: (i, 0))
    w_specs = [pl.BlockSpec((D, W_BLOCK), functools.partial(lambda b, i, c: (0, c), c=t * N_GROUPS + group),
                            pipeline_mode=pl.Buffered(1)) for t in range(3)]
    in_specs = [pl.BlockSpec((None, tm, D), lambda b, i: (b, i, 0)),
                pl.BlockSpec((1, D), lambda b, i: (0, 0))] + w_specs + [tab_spec] * 4
    args = [x, gain, wb, wb, wb]
    scratch = []
    if perm:
        src = np.arange(tm).reshape(tm // dil, dil).T.reshape(tm)
        onehot = np.zeros((tm, tm), np.float32)
        onehot[np.arange(tm), src] = 1.0
        args += [_class_major(t, tm, dil) for t in tabs] + [jnp.asarray(onehot, jnp.bfloat16)]
        in_specs.append(pl.BlockSpec((tm, tm), lambda b, i: (0, 0), pipeline_mode=pl.Buffered(1)))
    else:
        args += list(tabs)
        if dil > 1:
            scratch = [pltpu.VMEM((N_SLOTS, tm, HEAD_DIM), jnp.float32)]
    return pl.pallas_call(
        functools.partial(_proj_qkv_kernel, dil=dil, perm=perm),
        grid=(B, L // tm),
        in_specs=in_specs,
        out_specs=pl.BlockSpec((3, None, dil, tm // dil, ATTN_WIDTH), lambda b, i: (0, b, 0, i, 0)),
        out_shape=jax.ShapeDtypeStruct((3, B, dil, n, ATTN_WIDTH), jnp.bfloat16),
        scratch_shapes=scratch,
        compiler_params=_cparams(2),
        name=f"proj_qkv_d{dil}",
    )(*args)


def _rope_tables(L):
    inv_freq = jnp.power(ROPE_THETA, -jnp.arange(0, ROT_DIM, 2, dtype=jnp.float32) / ROT_DIM)
    ang = jnp.arange(L, dtype=jnp.float32)[:, None] * inv_freq[None, :]
    cos, sin = jnp.cos(ang), jnp.sin(ang)
    ones = jnp.ones((L, HEAD_DIM - ROT_DIM), jnp.float32)
    cos_t = jnp.concatenate([cos, cos, ones], axis=1)
    sin_t = jnp.concatenate([-sin, sin, 0.0 * ones], axis=1)
    scale = 1.0 / math.sqrt(HEAD_DIM)
    return cos_t * scale, sin_t * scale, cos_t, sin_t


def _attn_kernel(q_ref, kp_ref, kc_ref, kn_ref, vp_ref, vc_ref, vn_ref, o_ref, lse_ref, kw, vw, *, n):
    tq = q_ref.shape[0]
    i = pl.program_id(1)
    kw[0:RADIUS] = kp_ref[...]
    kw[RADIUS:RADIUS + tq] = kc_ref[...]
    kw[RADIUS + tq:] = kn_ref[...]
    vw[0:RADIUS] = vp_ref[...]
    vw[RADIUS:RADIUS + tq] = vc_ref[...]
    vw[RADIUS + tq:] = vn_ref[...]
    win = ATTN_SUB + 2 * RADIUS
    row = lax.broadcasted_iota(jnp.int32, (ATTN_SUB, win), 0)
    col = lax.broadcasted_iota(jnp.int32, (ATTN_SUB, win), 1)
    band = (col >= row) & (col <= row + 2 * RADIUS)
    lane = lax.broadcasted_iota(jnp.int32, (ATTN_SUB, LANES), 1)
    nsub = tq // ATTN_SUB
    for s in range(nsub):
        q0 = i * tq + s * ATTN_SUB
        valid = band
        if s == 0:
            valid = valid & (col >= RADIUS - q0)
        if s == nsub - 1:
            valid = valid & (col < n - q0 + RADIUS)
        lse_tile = jnp.zeros((ATTN_SUB, LANES), jnp.float32)
        for h in range(N_SLOTS):
            cs = slice(h * HEAD_DIM, (h + 1) * HEAD_DIM)
            qb = q_ref[s * ATTN_SUB:(s + 1) * ATTN_SUB, cs]
            kb = kw[s * ATTN_SUB:s * ATTN_SUB + win, cs]
            vb = vw[s * ATTN_SUB:s * ATTN_SUB + win, cs]
            sc = lax.dot_general(qb, kb, (((1,), (1,)), ((), ())), preferred_element_type=jnp.float32)
            sc = jnp.where(valid, sc, -1e30)
            m = jnp.max(sc, axis=1, keepdims=True)
            p = jnp.exp(sc - m)
            l = jnp.sum(p, axis=1, keepdims=True)
            o = jnp.dot(p.astype(jnp.bfloat16), vb, preferred_element_type=jnp.float32)
            o_ref[s * ATTN_SUB:(s + 1) * ATTN_SUB, cs] = o / l
            lse_tile = jnp.where(lane == h, m + jnp.log(l), lse_tile)
        lse_ref[s * ATTN_SUB:(s + 1) * ATTN_SUB, :] = lse_tile


def _attention(qkv):
    _, S, n, W = qkv.shape
    tq = min(ATTN_TQ, n)
    r = tq // RADIUS
    last = n // RADIUS - 1

    def cur(which):
        return pl.BlockSpec((None, None, tq, W), lambda s, i: (which, s, i, 0))

    def prev(which):
        return pl.BlockSpec((None, None, RADIUS, W), lambda s, i: (which, s, jnp.maximum(i * r - 1, 0), 0))

    def nxt(which):
        return pl.BlockSpec((None, None, RADIUS, W), lambda s, i: (which, s, jnp.minimum((i + 1) * r, last), 0))

    return pl.pallas_call(
        functools.partial(_attn_kernel, n=n),
        grid=(S, n // tq),
        in_specs=[cur(0), prev(1), cur(1), nxt(1), prev(2), cur(2), nxt(2)],
        out_specs=[pl.BlockSpec((None, tq, W), lambda s, i: (s, i, 0)),
                   pl.BlockSpec((None, tq, LANES), lambda s, i: (s, i, 0))],
        out_shape=[jax.ShapeDtypeStruct((S, n, W), jnp.float32),
                   jax.ShapeDtypeStruct((S, n, LANES), jnp.float32)],
        scratch_shapes=[pltpu.VMEM((tq + 2 * RADIUS, W), jnp.bfloat16),
                        pltpu.VMEM((tq + 2 * RADIUS, W), jnp.bfloat16)],
        compiler_params=_cparams(2),
        name="attn",
    )(qkv, qkv, qkv, qkv, qkv, qkv, qkv)


def _hyena_pre_kernel(u_ref, up_ref, un_ref, z_ref, cw_ref, cb_ref, g_ref, m_ref, scr_g, scr_m, *, n2, nblk):
    tl = u_ref.shape[0]
    i = pl.program_id(1)
    u = u_ref[...]
    has_prev = (i > 0).astype(jnp.float32)
    has_next = (i < nblk - 1).astype(jnp.float32)
    prev_row = up_ref[SUBLANES - 1:SUBLANES, :] * has_prev
    next_row = un_ref[0:1, :] * has_next
    ridx = lax.broadcasted_iota(jnp.int32, u.shape, 0)
    u_prev = jnp.where(ridx == 0, prev_row, pltpu.roll(u, 1, axis=0))
    u_next = jnp.where(ridx == tl - 1, next_row, pltpu.roll(u, tl - 1, axis=0))
    conv = u_prev * cw_ref[0:1, :] + u * cw_ref[1:2, :] + u_next * cw_ref[2:3, :] + cb_ref[...]
    C = HYENA_WIDTH
    x0, x1, vh = conv[:, :C], conv[:, C:2 * C], conv[:, 2 * C:]
    gated = vh * x1
    z = z_ref[...].astype(jnp.float32)
    mult = x0 * (z * _sigmoid(z))
    rows = tl // n2
    for c in range(C // LANES):
        scr_g[c] = gated[:, c * LANES:(c + 1) * LANES]
        scr_m[c] = mult[:, c * LANES:(c + 1) * LANES]
    for c in range(C // LANES):
        for r in range(n2):
            g_ref[r, :, c * LANES:(c + 1) * LANES] = scr_g[c, pl.ds(r, rows, stride=n2), :]
            m_ref[r, :, c * LANES:(c + 1) * LANES] = scr_m[c, pl.ds(r, rows, stride=n2), :]


def _hyena_pre(u_hy, gates, conv_w, conv_b, n2):
    B, L, C3 = u_hy.shape
    C = HYENA_WIDTH
    tl = HY_TL
    nblk = L // tl
    per = tl // SUBLANES
    last8 = L // SUBLANES - 1
    out_spec = pl.BlockSpec((None, n2, tl // n2, C), lambda b, i: (b, 0, i, 0))
    out_sds = jax.ShapeDtypeStruct((B, n2, L // n2, C), jnp.float32)
    return pl.pallas_call(
        functools.partial(_hyena_pre_kernel, n2=n2, nblk=nblk),
        grid=(B, nblk),
        in_specs=[pl.BlockSpec((None, tl, C3), lambda b, i: (b, i, 0)),
                  pl.BlockSpec((None, SUBLANES, C3), lambda b, i: (b, jnp.maximum(i * per - 1, 0), 0)),
                  pl.BlockSpec((None, SUBLANES, C3), lambda b, i: (b, jnp.minimum((i + 1) * per, last8), 0)),
                  pl.BlockSpec((None, tl, C), lambda b, i: (b, i, 1)),
                  pl.BlockSpec((3, C3), lambda b, i: (0, 0)),
                  pl.BlockSpec((1, C3), lambda b, i: (0, 0))],
        out_specs=[out_spec, out_spec],
        out_shape=[out_sds, out_sds],
        scratch_shapes=[pltpu.VMEM((C // LANES, tl, LANES), jnp.float32),
                        pltpu.VMEM((C // LANES, tl, LANES), jnp.float32)],
        compiler_params=_cparams(2),
        name="hyena_pre",
    )(u_hy, u_hy, u_hy, gates, conv_w, conv_b)


POS_VALID, POS_ZERO = FILTER_EMB, FILTER_EMB + 1


def _dot3(a, wh_ref, wl_ref):
    ah, al = _split_bf16(a)
    wh = wh_ref[...]
    return (jnp.dot(ah, wh, preferred_element_type=jnp.float32)
            + jnp.dot(al, wh, preferred_element_type=jnp.float32)
            + jnp.dot(ah, wl_ref[...], preferred_element_type=jnp.float32))


def _filter_kernel(p_ref, w1h, w1l, b1_ref, w2h, w2l, b2_ref, w3h, w3l, b3_ref, w4fh, w4fl, w4bh, w4bl,
                   fr_ref, dl_ref, sk_ref, o_ref):
    p = p_ref[...]
    fr = fr_ref[...]
    p2 = jnp.concatenate([p[:DFT_HALF], p[DFT_HALF:]], axis=1)
    hdn = jnp.sin(fr * (_dot3(p2, w1h, w1l) + b1_ref[...]))
    hdn = jnp.sin(fr * (_dot3(hdn, w2h, w2l) + b2_ref[...]))
    hdn = jnp.sin(fr * (_dot3(hdn, w3h, w3l) + b3_ref[...]))
    scale = jnp.exp(-p[:, 0:1] * dl_ref[...]) * p[:, POS_VALID:POS_VALID + 1]
    skip = p[:, POS_ZERO:POS_ZERO + 1] * sk_ref[...]
    for d, (wh, wl) in enumerate(((w4fh, w4fl), (w4bh, w4bl))):
        rs = slice(d * DFT_HALF, (d + 1) * DFT_HALF)
        o_ref[rs, :] = _dot3(hdn, wh, wl) * scale[rs] + skip[rs]


def _filter_lag_table(L, n2):
    bands = (FILTER_EMB - 1) // 2
    t = jnp.linspace(0.0, 1.0, L, dtype=jnp.float32)[:, None]
    w = 2.0 * math.pi * jnp.arange(L, dtype=jnp.float32)[:, None] / L
    f = jnp.linspace(1e-4, bands - 1, bands, dtype=jnp.float32)[None, :]
    z = jnp.concatenate([t, jnp.cos(f * w), -jnp.sin(f * w)], axis=-1)
    zn = jnp.concatenate([z, jnp.zeros((1, FILTER_EMB), jnp.float32), z[:0:-1]], axis=0)
    n = np.arange(2 * L)
    flags = np.zeros((2 * L, LANES - FILTER_EMB), np.float32)
    flags[:, 0] = n != L
    flags[:, 1] = n == 0
    tab = jnp.concatenate([zn, jnp.asarray(flags)], axis=1)
    return tab.reshape(2 * L // n2, n2, LANES).transpose(1, 0, 2).reshape(2 * L, LANES)


def _filter(L, n2, lp):
    C = HYENA_WIDTH
    tab = _filter_lag_table(L, n2)
    H = FILTER_HIDDEN
    assert 2 * H == LANES
    zeros = lambda r, c: jnp.zeros((r, c), jnp.float32)
    blockdiag = lambda w: jnp.block([[w, zeros(*w.shape)], [zeros(*w.shape), w]])
    w1p = jnp.concatenate([lp["filt_w1"], zeros(LANES - FILTER_EMB, H)], axis=0)
    w4 = lp["filt_w4"]
    w4f = jnp.concatenate([w4[:, :C], zeros(H, C)], axis=0)
    w4b = jnp.concatenate([zeros(H, C), w4[:, C:]], axis=0)
    twice = lambda v: jnp.concatenate([v, v])[None, :]
    min_decay = math.log(DECAY_TARGET) / DECAY_FAST_PCT
    max_decay = math.log(DECAY_TARGET) / DECAY_SLOW_PCT
    deltas = jnp.abs(jnp.linspace(min_decay, max_decay, C, dtype=jnp.float32))[None, :]
    full = lambda a: pl.BlockSpec(a.shape, lambda i: (0,) * a.ndim)
    args = [*_split_bf16(blockdiag(w1p)), twice(lp["filt_b1"]), *_split_bf16(blockdiag(lp["filt_w2"])),
            twice(lp["filt_b2"]), *_split_bf16(blockdiag(lp["filt_w3"])), twice(lp["filt_b3"]),
            *_split_bf16(w4f), *_split_bf16(w4b), twice(lp["filt_freq"]), deltas, lp["hyena_skip"][None, :]]
    k = pl.pallas_call(
        _filter_kernel,
        grid=(n2,),
        in_specs=[pl.BlockSpec((DFT_N1, LANES), lambda i: (i, 0))] + [full(a) for a in args],
        out_specs=pl.BlockSpec((DFT_N1, C), lambda i: (i, 0)),
        out_shape=jax.ShapeDtypeStruct((2 * L, C), jnp.float32),
        compiler_params=_cparams(1),
        name="filter",
    )(tab, *args)
    return k.reshape(1, n2, DFT_N1, C)


def _dft_tables(n2):
    N = DFT_N1 * n2
    k1 = np.arange(DFT_ROWS)
    live = (k1 <= DFT_HALF).astype(np.float64)
    a = 2.0 * np.pi * ((k1[:, None] * np.arange(DFT_N1)[None, :]) % DFT_N1) / DFT_N1
    b = 2.0 * np.pi * (k1[None, :] * np.arange(n2)[:, None]) / N
    ca, sa = (jnp.asarray(t * live[:, None], jnp.float32) for t in (np.cos(a), np.sin(a)))
    cb, sb = (jnp.asarray(t, jnp.float32) for t in (np.cos(b), np.sin(b)))
    f_re = ca[None] * cb[:, :, None] - sa[None] * sb[:, :, None]
    f_im = -(sa[None] * cb[:, :, None] + ca[None] * sb[:, :, None])
    F = jnp.concatenate([f_re, f_im], axis=1).astype(jnp.bfloat16)
    weight = np.where((k1 == 0) | (k1 == DFT_HALF), 1.0, 2.0) / N
    cat, sat = (jnp.asarray((t * live[:, None] * weight[:, None]).T[:DFT_HALF], jnp.float32)
                for t in (np.cos(a), np.sin(a)))
    g_re = cat[None] * cb[:, None, :] - sat[None] * sb[:, None, :]
    g_im = -(sat[None] * cb[:, None, :] + cat[None] * sb[:, None, :])
    G = jnp.concatenate([g_re, g_im], axis=2).astype(jnp.bfloat16)
    return F, G


def _dft_mm_kernel(x_ref, m_ref, *rest, n_slabs):
    o_ref = rest[-1]
    g_ref = rest[0] if len(rest) == 2 else None
    for s in range(n_slabs):
        y = jnp.dot(m_ref[s], x_ref[s].astype(jnp.bfloat16), preferred_element_type=jnp.float32)
        if g_ref is not None:
            y = y * g_ref[s]
        o_ref[s] = y


def _dft_mm(x, mat, k_used, gate=None, tc=256, slabs=4):
    Bx, n2, K, C = x.shape
    assert K == k_used
    M = mat.shape[1]
    slabs = min(slabs, n2)
    x_spec = pl.BlockSpec((None, slabs, K, tc), lambda c, b, r: (b, r, 0, c))
    m_spec = pl.BlockSpec((slabs, M, K), lambda c, b, r: (r, 0, 0))
    o_spec = pl.BlockSpec((None, slabs, M, tc), lambda c, b, r: (b, r, 0, c))
    in_specs = [x_spec, m_spec]
    args = [x, mat]
    if gate is not None:
        in_specs.append(o_spec)
        args.append(gate)
    return pl.pallas_call(
        functools.partial(_dft_mm_kernel, n_slabs=slabs),
        grid=(C // tc, Bx, n2 // slabs),
        in_specs=in_specs,
        out_specs=o_spec,
        out_shape=jax.ShapeDtypeStruct((Bx, n2, M, C), jnp.float32),
        compiler_params=_cparams(3),
        name="dft_mm",
    )(*args)


def _fft_list(xs, sign):
    n = len(xs)
    if n == 1:
        return xs
    even = _fft_list(xs[0::2], sign)
    odd = _fft_list(xs[1::2], sign)
    out = [None] * n
    for k in range(n // 2):
        er, ei = even[k]
        pr, pi = odd[k]
        if k == 0:
            tr, ti = pr, pi
        elif 4 * k == n:
            tr, ti = (pi, -pr) if sign < 0 else (-pi, pr)
        else:
            ang = sign * 2.0 * math.pi * k / n
            wr, wi = math.cos(ang), math.sin(ang)
            tr, ti = pr * wr - pi * wi, pr * wi + pi * wr
        out[k] = (er + tr, ei + ti)
        out[k + n // 2] = (er - tr, ei - ti)
    return out


def _dft_mid_kernel(s_ref, *rest, n2, with_filter):
    o_ref = rest[-1]
    kf_ref = rest[0] if with_filter else None
    rows = s_ref.shape[2]

    def body(c, carry):
        rs = pl.ds(pl.multiple_of(c * SUBLANES, SUBLANES), SUBLANES)
        xs = [(s_ref[r, 0, rs, :], s_ref[r, 1, rs, :]) for r in range(n2)]
        ys = _fft_list(xs, -1)
        if with_filter:
            prod = []
            for k2 in range(n2):
                yr, yi = ys[k2]
                fr, fi = kf_ref[k2, 0, rs, :], kf_ref[k2, 1, rs, :]
                prod.append((yr * fr - yi * fi, yr * fi + yi * fr))
            ys = _fft_list(prod, +1)
        for r in range(n2):
            o_ref[r, 0, rs, :] = ys[r][0]
            o_ref[r, 1, rs, :] = ys[r][1]
        return carry

    lax.fori_loop(0, rows // SUBLANES, body, 0)


def _dft_mid(s, kf=None, tc=128, tr=88):
    Bx, n2, M, C = s.shape
    s5 = s.reshape(Bx, n2, 2, DFT_ROWS, C)
    blk = pl.BlockSpec((None, n2, 2, tr, tc), lambda c, r, b: (b, 0, 0, r, c))
    in_specs, args = [blk], [s5]
    if kf is not None:
        in_specs.append(pl.BlockSpec((n2, 2, tr, tc), lambda c, r, b: (0, 0, r, c)))
        args.append(kf.reshape(n2, 2, DFT_ROWS, C))
    out = pl.pallas_call(
        functools.partial(_dft_mid_kernel, n2=n2, with_filter=kf is not None),
        grid=(C // tc, DFT_ROWS // tr, Bx),
        in_specs=in_specs,
        out_specs=blk,
        out_shape=jax.ShapeDtypeStruct((Bx, n2, 2, DFT_ROWS, C), jnp.float32),
        compiler_params=_cparams(3),
        name="dft_mid",
    )(*args)
    return out.reshape(Bx, n2, M, C)


def _regroup_to_tokens(blk_ref, scr, dil, width):
    m = blk_ref.shape[1]
    for c in range(width // LANES):
        for r in range(dil):
            scr[c, pl.ds(r, m, stride=dil), :] = blk_ref[r, :, c * LANES:(c + 1) * LANES]


def _merge_kernel(o0_ref, l0_ref, o1_ref, l1_ref, o2_ref, l2_ref, hy_ref, za_ref, ga_ref, gh_ref,
                  wa_ref, wh_ref, out_ref, so1, so2, sl1, sl2, shy, attn_scr, hy_scr, *, n2):
    d1, d2 = ATTN_GROUPS[1][1], ATTN_GROUPS[2][1]
    _regroup_to_tokens(o1_ref, so1, d1, ATTN_WIDTH)
    _regroup_to_tokens(o2_ref, so2, d2, ATTN_WIDTH)
    _regroup_to_tokens(l1_ref, sl1, d1, LANES)
    _regroup_to_tokens(l2_ref, sl2, d2, LANES)
    _regroup_to_tokens(hy_ref, shy, n2, HYENA_WIDTH)
    l0, l1, l2 = l0_ref[...], sl1[0], sl2[0]
    mx = jnp.maximum(jnp.maximum(l0, l1), l2)
    e0, e1, e2 = jnp.exp(l0 - mx), jnp.exp(l1 - mx), jnp.exp(l2 - mx)
    inv = 1.0 / (e0 + e1 + e2)
    a0, a1, a2 = e0 * inv, e1 * inv, e2 * inv
    for h in range(N_SLOTS):
        cs = slice(h * HEAD_DIM, (h + 1) * HEAD_DIM)
        mix = a0[:, h:h + 1] * o0_ref[:, cs] + a1[:, h:h + 1] * so1[h] + a2[:, h:h + 1] * so2[h]
        z = za_ref[:, cs].astype(jnp.float32)
        attn_scr[:, cs] = (mix * (z * _sigmoid(z))).astype(jnp.bfloat16)
        hy_scr[:, cs] = shy[h].astype(jnp.bfloat16)
    br_a = jnp.dot(attn_scr[...], wa_ref[...], preferred_element_type=jnp.float32)
    br_h = jnp.dot(hy_scr[...], wh_ref[...], preferred_element_type=jnp.float32)
    merged = (_sigmoid(ga_ref[...].astype(jnp.float32)) * br_a
              + _sigmoid(gh_ref[...].astype(jnp.float32)) * br_h)
    out_ref[...] = merged.astype(out_ref.dtype)


def _merge(outs, lses, hyena, gates, w_ba, w_bh, n2):
    B, d0, L, W = outs[0].shape
    D = w_ba.shape[1]
    tm = MERGE_TM
    d1, d2 = ATTN_GROUPS[1][1], ATTN_GROUPS[2][1]

    def cls(d, width):
        return pl.BlockSpec((None, d, tm // d, width), lambda b, i: (b, 0, i, 0))

    in_specs = [pl.BlockSpec((None, None, tm, W), lambda b, i: (b, 0, i, 0)),
                pl.BlockSpec((None, None, tm, LANES), lambda b, i: (b, 0, i, 0)),
                cls(d1, W), cls(d1, LANES), cls(d2, W), cls(d2, LANES), cls(n2, HYENA_WIDTH),
                pl.BlockSpec((None, tm, W), lambda b, i: (b, i, 0)),
                pl.BlockSpec((None, tm, D), lambda b, i: (b, i, 1)),
                pl.BlockSpec((None, tm, D), lambda b, i: (b, i, 2)),
                pl.BlockSpec(w_ba.shape, lambda b, i: (0, 0), pipeline_mode=pl.Buffered(1)),
                pl.BlockSpec(w_bh.shape, lambda b, i: (0, 0), pipeline_mode=pl.Buffered(1))]
    wt = W // LANES
    return pl.pallas_call(
        functools.partial(_merge_kernel, n2=n2),
        grid=(B, L // tm),
        in_specs=in_specs,
        out_specs=pl.BlockSpec((None, tm, D), lambda b, i: (b, i, 0)),
        out_shape=jax.ShapeDtypeStruct((B, L, D), jnp.bfloat16),
        scratch_shapes=[pltpu.VMEM((wt, tm, LANES), jnp.float32), pltpu.VMEM((wt, tm, LANES), jnp.float32),
                        pltpu.VMEM((1, tm, LANES), jnp.float32), pltpu.VMEM((1, tm, LANES), jnp.float32),
                        pltpu.VMEM((wt, tm, LANES), jnp.float32),
                        pltpu.VMEM((tm, W), jnp.bfloat16), pltpu.VMEM((tm, HYENA_WIDTH), jnp.bfloat16)],
        compiler_params=_cparams(2),
        name="merge",
    )(outs[0], lses[0], outs[1], lses[1], outs[2], lses[2], hyena, gates, gates, gates, w_ba, w_bh)


def _out_kernel(m_ref, w_ref, x_ref, g_ref, o_ref, *, final_norm):
    y = x_ref[...] + jnp.dot(m_ref[...], w_ref[...], preferred_element_type=jnp.float32)
    if final_norm:
        ms = jnp.mean(y * y, axis=-1, keepdims=True)
        y = y * lax.rsqrt(ms + RMS_EPS) * g_ref[...]
    o_ref[...] = y


def _out_proj(merged, w_out, x, final_gain, final_norm):
    B, L, D = x.shape
    tm = MERGE_TM
    tok = pl.BlockSpec((None, tm, D), lambda b, i: (b, i, 0))
    return pl.pallas_call(
        functools.partial(_out_kernel, final_norm=final_norm),
        grid=(B, L // tm),
        in_specs=[tok, pl.BlockSpec(w_out.shape, lambda b, i: (0, 0), pipeline_mode=pl.Buffered(1)),
                  tok, pl.BlockSpec((1, D), lambda b, i: (0, 0))],
        out_specs=tok,
        out_shape=jax.ShapeDtypeStruct((B, L, D), jnp.float32),
        compiler_params=_cparams(2),
        name="out_proj",
    )(merged, w_out, x, final_gain)


def _prep_layer(i, norm_gain, w_in, conv_w, conv_b, filt_w1, filt_b1, filt_w2, filt_b2, filt_w3, filt_b3,
                filt_w4, filt_freq, hyena_skip, w_branch_attn, w_branch_hyena, w_out):
    bf = jnp.bfloat16
    return dict(
        gain=norm_gain[i][None, :],
        w_in=w_in[i].astype(bf),
        conv_w=conv_w[i], conv_b=conv_b[i][None, :],
        filt_w1=filt_w1[i], filt_b1=filt_b1[i], filt_w2=filt_w2[i], filt_b2=filt_b2[i],
        filt_w3=filt_w3[i], filt_b3=filt_b3[i], filt_w4=filt_w4[i], filt_freq=filt_freq[i],
        hyena_skip=hyena_skip[i],
        w_ba=w_branch_attn[i].astype(bf), w_bh=w_branch_hyena[i].astype(bf), w_out=w_out[i].astype(bf))


def _layer(x, lp, final_gain, final_norm):
    B, L, D = x.shape
    assert L % (2 * DFT_HALF) == 0 and L % PROJ_TM == 0 and D == 2 * W_BLOCK
    n2 = L // DFT_HALF

    tabs = _rope_tables(L)
    outs, lses = [], []
    for g, (_, dil) in enumerate(ATTN_GROUPS):
        qkv = _proj_qkv(x, lp["gain"], lp["w_in"], tabs, dil, g)
        n = L // dil
        o, lse = _attention(qkv.reshape(3, B * dil, n, ATTN_WIDTH))
        outs.append(o.reshape(B, dil, n, ATTN_WIDTH))
        lses.append(lse.reshape(B, dil, n, LANES))

    gate_cols = (lambda s: jnp.where(s == 0, GATE_BLOCKS[0], GATE_BLOCKS[2] + 2 * (s - 1)),
                 lambda s: jnp.where(s == 0, GATE_BLOCKS[1], GATE_BLOCKS[2] + 2 * (s - 1) + 1))
    gates = _proj_plain(x, lp["gain"], lp["w_in"], gate_cols, 3, jnp.bfloat16)
    uhy_cols = tuple(functools.partial(lambda s, c: c, c=UHY_BLOCK + k) for k in range(3))
    u_hy = _proj_plain(x, lp["gain"], lp["w_in"], uhy_cols, 1, jnp.float32)
    gated, mult = _hyena_pre(u_hy, gates, lp["conv_w"], lp["conv_b"], n2)
    fwd_tabs, inv_tabs = _dft_tables(n2)
    kf = _dft_mid(_dft_mm(_filter(L, n2, lp), fwd_tabs, DFT_N1))[0]
    spec = _dft_mm(gated, fwd_tabs, DFT_HALF)
    spec = _dft_mid(spec, kf)
    hyena = _dft_mm(spec, inv_tabs, DFT_M, gate=mult)

    merged = _merge(outs, lses, hyena, gates, lp["w_ba"], lp["w_bh"], n2)
    return _out_proj(merged, lp["w_out"], x, final_gain, final_norm)


def kernel(x_prompt, x_sample, norm_gain, w_in, conv_w, conv_b, filt_w1, filt_b1, filt_w2, filt_b2, filt_w3, filt_b3, filt_w4, filt_freq, hyena_skip, w_branch_attn, w_branch_hyena, w_out, final_gain):
    depth = w_in.shape[0]
    layers = [_prep_layer(i, norm_gain, w_in, conv_w, conv_b, filt_w1, filt_b1, filt_w2, filt_b2, filt_w3,
                          filt_b3, filt_w4, filt_freq, hyena_skip, w_branch_attn, w_branch_hyena, w_out)
              for i in range(depth)]
    fg = final_gain[None, :]

    def trunk(x):
        for i, lp in enumerate(layers):
            x = _layer(x, lp, fg, final_norm=(i == depth - 1))
        return x

    return (trunk(x_prompt), trunk(x_sample))
```

```python
import functools
import math

import jax
import jax.numpy as jnp
import numpy as np
from jax import lax
from jax.experimental import pallas as pl
from jax.experimental.pallas import tpu as pltpu

HEAD_DIM = 128
N_SLOTS = 8
ATTN_GROUPS = ((128, 1), (512, 4), (2048, 16))
N_GROUPS = len(ATTN_GROUPS)
ATTN_WIDTH = N_SLOTS * HEAD_DIM
QKV_WIDTH = N_GROUPS * ATTN_WIDTH
ROT_DIM = HEAD_DIM // 4
ROPE_THETA = 500000.0
HYENA_WIDTH = 1024
FILTER_EMB = 33
FILTER_HIDDEN = 64
DECAY_TARGET = 1e-2
DECAY_FAST_PCT = 0.3
DECAY_SLOW_PCT = 1.5
RMS_EPS = 1e-6
RADIUS = 64

LANES = 128
SUBLANES = 8
VMEM_LIMIT = 56 * 1024 * 1024
LONG_CONV_VMEM = 36 * 1024 * 1024

DFT_N1 = 512
DFT_HALF = DFT_N1 // 2
DFT_ROWS = 264
DFT_M = 2 * DFT_ROWS

PROJ_TM = 512
W_BLOCK = 1024
_ZA_BLOCK = 3 * QKV_WIDTH // W_BLOCK
UHY_BLOCK = _ZA_BLOCK + ATTN_WIDTH // W_BLOCK
_ZH_BLOCK = UHY_BLOCK + 3 * HYENA_WIDTH // W_BLOCK
GATE_BLOCKS = (_ZA_BLOCK, _ZH_BLOCK, _ZH_BLOCK + HYENA_WIDTH // W_BLOCK)
PERM_MIN_DIL = 8
ATTN_TQ = 512
ATTN_SUB = 128
HY_TL = 512
MERGE_TM = 512


def _cparams(n_axes):
    return pltpu.CompilerParams(dimension_semantics=("parallel",) * n_axes,
                                vmem_limit_bytes=VMEM_LIMIT)


def _sigmoid(x):
    return 0.5 * jnp.tanh(0.5 * x) + 0.5


def _split_bf16(x):
    hi = x.astype(jnp.bfloat16)
    lo = (x - hi.astype(jnp.float32)).astype(jnp.bfloat16)
    return hi, lo


def _normed(x_ref, g_ref):
    x = x_ref[...]
    ms = jnp.mean(x * x, axis=-1, keepdims=True)
    return (x * lax.rsqrt(ms + RMS_EPS) * g_ref[...]).astype(jnp.bfloat16)


def _proj_plain_kernel(x_ref, g_ref, *rest):
    w_refs, o_ref = rest[:-1], rest[-1]
    h = _normed(x_ref, g_ref)
    for k, w_ref in enumerate(w_refs):
        cw = w_ref.shape[1]
        o_ref[:, k * cw:(k + 1) * cw] = jnp.dot(h, w_ref[...], preferred_element_type=jnp.float32).astype(o_ref.dtype)


def _proj_plain(x, gain, wb, col_maps, n_slabs, out_dtype):
    B, L, D = x.shape
    tm = PROJ_TM
    cs = len(col_maps) * W_BLOCK
    w_specs = [pl.BlockSpec((D, W_BLOCK), functools.partial(lambda s, b, i, f: (0, f(s)), f=f),
                            pipeline_mode=pl.Buffered(1)) for f in col_maps]
    return pl.pallas_call(
        _proj_plain_kernel,
        grid=(n_slabs, B, L // tm),
        in_specs=[pl.BlockSpec((None, tm, D), lambda s, b, i: (b, i, 0)),
                  pl.BlockSpec((1, D), lambda s, b, i: (0, 0))] + w_specs,
        out_specs=pl.BlockSpec((None, tm, cs), lambda s, b, i: (b, i, s)),
        out_shape=jax.ShapeDtypeStruct((B, L, n_slabs * cs), out_dtype),
        compiler_params=_cparams(3),
        name="proj_plain",
    )(x, gain, *([wb] * len(col_maps)))


def _proj_qkv_kernel(x_ref, g_ref, wq_ref, wk_ref, wv_ref, cq_ref, sq_ref, ck_ref, sk_ref, *rest, dil, perm):
    tm = x_ref.shape[0]
    rows = tm // dil
    h = _normed(x_ref, g_ref)
    if perm:
        p_ref, o_ref = rest
        h = jnp.dot(p_ref[...], h, preferred_element_type=jnp.float32).astype(jnp.bfloat16)
    elif dil > 1:
        o_ref, scr = rest
    else:
        (o_ref,) = rest
    lane = lax.broadcasted_iota(jnp.int32, (tm, HEAD_DIM), 1)
    first_half = lane < (ROT_DIM // 2)
    for which, w_ref in enumerate((wq_ref, wk_ref, wv_ref)):
        acc = jnp.dot(h, w_ref[...], preferred_element_type=jnp.float32)
        for slot in range(N_SLOTS):
            cs = slice(slot * HEAD_DIM, (slot + 1) * HEAD_DIM)
            t = acc[:, cs]
            if which < 2:
                cos_ref, sin_ref = (cq_ref, sq_ref) if which == 0 else (ck_ref, sk_ref)
                partner = jnp.where(first_half,
                                    pltpu.roll(t, HEAD_DIM - ROT_DIM // 2, axis=1),
                                    pltpu.roll(t, ROT_DIM // 2, axis=1))
                t = t * cos_ref[...] + partner * sin_ref[...]
            if perm or dil == 1:
                for r in range(dil):
                    o_ref[which, r, :, cs] = t[r * rows:(r + 1) * rows].astype(o_ref.dtype)
            else:
                scr[slot] = t
        if not perm and dil > 1:
            for slot in range(N_SLOTS):
                for r in range(dil):
                    o_ref[which, r, :, slot * HEAD_DIM:(slot + 1) * HEAD_DIM] = (
                        scr[slot, pl.ds(r, rows, stride=dil), :].astype(o_ref.dtype))


def _class_major(a, tm, dil):
    n_tiles = a.shape[0] // tm
    return a.reshape(n_tiles, tm // dil, dil, -1).transpose(0, 2, 1, 3).reshape(a.shape)


def _class_major_onehot(tm, dil):
    src = np.arange(tm).reshape(tm // dil, dil).T.reshape(tm)
    onehot = np.zeros((tm, tm), np.float32)
    onehot[np.arange(tm), src] = 1.0
    return jnp.asarray(onehot, jnp.bfloat16)


def _proj_qkv(x, gain, wb, tabs, dil, group):
    B, L, D = x.shape
    tm = PROJ_TM
    n = L // dil
    perm = dil >= PERM_MIN_DIL
    tab_spec = pl.BlockSpec((tm, HEAD_DIM), lambda b, i: (i, 0))
    w_specs = [pl.BlockSpec((D, W_BLOCK), functools.partial(lambda b, i, c: (0, c), c=t * N_GROUPS + group),
                            pipeline_mode=pl.Buffered(1)) for t in range(3)]
    in_specs = [pl.BlockSpec((None, tm, D), lambda b, i: (b, i, 0)),
                pl.BlockSpec((1, D), lambda b, i: (0, 0))] + w_specs + [tab_spec] * 4
    args = [x, gain, wb, wb, wb]
    scratch = []
    if perm:
        args += [_class_major(t, tm, dil) for t in tabs] + [_class_major_onehot(tm, dil)]
        in_specs.append(pl.BlockSpec((tm, tm), lambda b, i: (0, 0), pipeline_mode=pl.Buffered(1)))
    else:
        args += list(tabs)
        if dil > 1:
            scratch = [pltpu.VMEM((N_SLOTS, tm, HEAD_DIM), jnp.float32)]
    return pl.pallas_call(
        functools.partial(_proj_qkv_kernel, dil=dil, perm=perm),
        grid=(B, L // tm),
        in_specs=in_specs,
        out_specs=pl.BlockSpec((3, None, dil, tm // dil, ATTN_WIDTH), lambda b, i: (0, b, 0, i, 0)),
        out_shape=jax.ShapeDtypeStruct((3, B, dil, n, ATTN_WIDTH), jnp.bfloat16),
        scratch_shapes=scratch,
        compiler_params=_cparams(2),
        name=f"proj_qkv_d{dil}",
    )(*args)


def _rope_tables(L):
    inv_freq = jnp.power(ROPE_THETA, -jnp.arange(0, ROT_DIM, 2, dtype=jnp.float32) / ROT_DIM)
    ang = jnp.arange(L, dtype=jnp.float32)[:, None] * inv_freq[None, :]
    cos, sin = jnp.cos(ang), jnp.sin(ang)
    ones = jnp.ones((L, HEAD_DIM - ROT_DIM), jnp.float32)
    cos_t = jnp.concatenate([cos, cos, ones], axis=1)
    sin_t = jnp.concatenate([-sin, sin, 0.0 * ones], axis=1)
    scale = 1.0 / math.sqrt(HEAD_DIM)
    return cos_t * scale, sin_t * scale, cos_t, sin_t


def _attn_kernel(q_ref, kp_ref, kc_ref, kn_ref, vp_ref, vc_ref, vn_ref, o_ref, lse_ref, kw, vw, *, n):
    tq = q_ref.shape[0]
    i = pl.program_id(1)
    kw[0:RADIUS] = kp_ref[...]
    kw[RADIUS:RADIUS + tq] = kc_ref[...]
    kw[RADIUS + tq:] = kn_ref[...]
    vw[0:RADIUS] = vp_ref[...]
    vw[RADIUS:RADIUS + tq] = vc_ref[...]
    vw[RADIUS + tq:] = vn_ref[...]
    win = ATTN_SUB + 2 * RADIUS
    row = lax.broadcasted_iota(jnp.int32, (ATTN_SUB, win), 0)
    col = lax.broadcasted_iota(jnp.int32, (ATTN_SUB, win), 1)
    band = (col >= row) & (col <= row + 2 * RADIUS)
    lane = lax.broadcasted_iota(jnp.int32, (ATTN_SUB, LANES), 1)
    nsub = tq // ATTN_SUB
    for s in range(nsub):
        q0 = i * tq + s * ATTN_SUB
        valid = band
        if s == 0:
            valid = valid & (col >= RADIUS - q0)
        if s == nsub - 1:
            valid = valid & (col < n - q0 + RADIUS)
        lse_tile = jnp.zeros((ATTN_SUB, LANES), jnp.float32)
        for h in range(N_SLOTS):
            cs = slice(h * HEAD_DIM, (h + 1) * HEAD_DIM)
            qb = q_ref[s * ATTN_SUB:(s + 1) * ATTN_SUB, cs]
            kb = kw[s * ATTN_SUB:s * ATTN_SUB + win, cs]
            vb = vw[s * ATTN_SUB:s * ATTN_SUB + win, cs]
            sc = lax.dot_general(qb, kb, (((1,), (1,)), ((), ())), preferred_element_type=jnp.float32)
            sc = jnp.where(valid, sc, -1e30)
            m = jnp.max(sc, axis=1, keepdims=True)
            p = jnp.exp(sc - m)
            l = jnp.sum(p, axis=1, keepdims=True)
            o = jnp.dot(p.astype(jnp.bfloat16), vb, preferred_element_type=jnp.float32)
            o_ref[s * ATTN_SUB:(s + 1) * ATTN_SUB, cs] = o / l
            lse_tile = jnp.where(lane == h, m + jnp.log(l), lse_tile)
        lse_ref[s * ATTN_SUB:(s + 1) * ATTN_SUB, :] = lse_tile


def _attention(qkv):
    _, S, n, W = qkv.shape
    tq = min(ATTN_TQ, n)
    r = tq // RADIUS
    last = n // RADIUS - 1

    def cur(which):
        return pl.BlockSpec((None, None, tq, W), lambda s, i: (which, s, i, 0))

    def prev(which):
        return pl.BlockSpec((None, None, RADIUS, W), lambda s, i: (which, s, jnp.maximum(i * r - 1, 0), 0))

    def nxt(which):
        return pl.BlockSpec((None, None, RADIUS, W), lambda s, i: (which, s, jnp.minimum((i + 1) * r, last), 0))

    return pl.pallas_call(
        functools.partial(_attn_kernel, n=n),
        grid=(S, n // tq),
        in_specs=[cur(0), prev(1), cur(1), nxt(1), prev(2), cur(2), nxt(2)],
        out_specs=[pl.BlockSpec((None, tq, W), lambda s, i: (s, i, 0)),
                   pl.BlockSpec((None, tq, LANES), lambda s, i: (s, i, 0))],
        out_shape=[jax.ShapeDtypeStruct((S, n, W), jnp.float32),
                   jax.ShapeDtypeStruct((S, n, LANES), jnp.float32)],
        scratch_shapes=[pltpu.VMEM((tq + 2 * RADIUS, W), jnp.bfloat16),
                        pltpu.VMEM((tq + 2 * RADIUS, W), jnp.bfloat16)],
        compiler_params=_cparams(2),
        name="attn",
    )(qkv, qkv, qkv, qkv, qkv, qkv, qkv)


def _shift_rows(u, edge_row, down):
    tl = u.shape[0]
    rolled = pltpu.roll(u, 1 if down else tl - 1, axis=0)
    r8 = lax.broadcasted_iota(jnp.int32, (SUBLANES, u.shape[1]), 0)
    if down:
        head = jnp.where(r8 == 0, edge_row, rolled[:SUBLANES])
        return jnp.concatenate([head, rolled[SUBLANES:]], axis=0)
    tail = jnp.where(r8 == SUBLANES - 1, edge_row, rolled[tl - SUBLANES:])
    return jnp.concatenate([rolled[:tl - SUBLANES], tail], axis=0)


def _hyena_pre_kernel(u_ref, up_ref, un_ref, z_ref, cw_ref, cb_ref, p_ref, g_ref, m_ref, *, n2, nblk):
    tl = u_ref.shape[0]
    i = pl.program_id(1)
    has_prev = (i > 0).astype(jnp.float32)
    has_next = (i < nblk - 1).astype(jnp.float32)
    C = HYENA_WIDTH
    rows = tl // n2
    perm = p_ref[...]

    def conv(cs):
        u = u_ref[:, cs]
        u_prev = _shift_rows(u, up_ref[SUBLANES - 1:SUBLANES, cs] * has_prev, True)
        u_next = _shift_rows(u, un_ref[0:1, cs] * has_next, False)
        return u_prev * cw_ref[0:1, cs] + u * cw_ref[1:2, cs] + u_next * cw_ref[2:3, cs] + cb_ref[:, cs]

    for c in range(C // LANES):
        ls = slice(c * LANES, (c + 1) * LANES)
        x0, x1, vh = (conv(slice(k * C + c * LANES, k * C + (c + 1) * LANES)) for k in range(3))
        z = z_ref[:, ls].astype(jnp.float32)
        m_hi, m_lo = _split_bf16(x0 * (z * _sigmoid(z)))
        stack = jnp.concatenate([(vh * x1).astype(jnp.bfloat16), m_hi, m_lo], axis=1)
        res = jnp.dot(perm, stack, preferred_element_type=jnp.float32)
        g = res[:, :LANES]
        m = res[:, LANES:2 * LANES] + res[:, 2 * LANES:]
        for r in range(n2):
            g_ref[r, :, ls] = g[r * rows:(r + 1) * rows]
            m_ref[r, :, ls] = m[r * rows:(r + 1) * rows]


def _hyena_pre(u_hy, gates, conv_w, conv_b, n2):
    B, L, C3 = u_hy.shape
    C = HYENA_WIDTH
    tl = HY_TL
    nblk = L // tl
    per = tl // SUBLANES
    last8 = L // SUBLANES - 1
    out_spec = pl.BlockSpec((None, n2, tl // n2, C), lambda b, i: (b, 0, i, 0))
    out_sds = jax.ShapeDtypeStruct((B, n2, L // n2, C), jnp.float32)
    return pl.pallas_call(
        functools.partial(_hyena_pre_kernel, n2=n2, nblk=nblk),
        grid=(B, nblk),
        in_specs=[pl.BlockSpec((None, tl, C3), lambda b, i: (b, i, 0)),
                  pl.BlockSpec((None, SUBLANES, C3), lambda b, i: (b, jnp.maximum(i * per - 1, 0), 0)),
                  pl.BlockSpec((None, SUBLANES, C3), lambda b, i: (b, jnp.minimum((i + 1) * per, last8), 0)),
                  pl.BlockSpec((None, tl, C), lambda b, i: (b, i, 1)),
                  pl.BlockSpec((3, C3), lambda b, i: (0, 0)),
                  pl.BlockSpec((1, C3), lambda b, i: (0, 0)),
                  pl.BlockSpec((tl, tl), lambda b, i: (0, 0), pipeline_mode=pl.Buffered(1))],
        out_specs=[out_spec, out_spec],
        out_shape=[out_sds, out_sds],
        compiler_params=_cparams(2),
        name="hyena_pre",
    )(u_hy, u_hy, u_hy, gates, conv_w, conv_b, _class_major_onehot(tl, n2))


POS_VALID, POS_ZERO = FILTER_EMB, FILTER_EMB + 1


def _dot3(a, wh_ref, wl_ref):
    ah, al = _split_bf16(a)
    wh = wh_ref[...]
    return (jnp.dot(ah, wh, preferred_element_type=jnp.float32)
            + jnp.dot(al, wh, preferred_element_type=jnp.float32)
            + jnp.dot(ah, wl_ref[...], preferred_element_type=jnp.float32))


def _filter_kernel(p_ref, w1h, w1l, b1_ref, w2h, w2l, b2_ref, w3h, w3l, b3_ref, w4fh, w4fl, w4bh, w4bl,
                   fr_ref, dl_ref, sk_ref, o_ref):
    p = p_ref[...]
    fr = fr_ref[...]
    p2 = jnp.concatenate([p[:DFT_HALF], p[DFT_HALF:]], axis=1)
    hdn = jnp.sin(fr * (_dot3(p2, w1h, w1l) + b1_ref[...]))
    hdn = jnp.sin(fr * (_dot3(hdn, w2h, w2l) + b2_ref[...]))
    hdn = jnp.sin(fr * (_dot3(hdn, w3h, w3l) + b3_ref[...]))
    scale = jnp.exp(-p[:, 0:1] * dl_ref[...]) * p[:, POS_VALID:POS_VALID + 1]
    skip = p[:, POS_ZERO:POS_ZERO + 1] * sk_ref[...]
    for d, (wh, wl) in enumerate(((w4fh, w4fl), (w4bh, w4bl))):
        rs = slice(d * DFT_HALF, (d + 1) * DFT_HALF)
        o_ref[rs, :] = (_dot3(hdn, wh, wl) * scale[rs] + skip[rs]).astype(o_ref.dtype)


def _filter_lag_table(L, n2):
    bands = (FILTER_EMB - 1) // 2
    n = jnp.asarray(np.arange(2 * L).reshape(2 * L // n2, n2).T.reshape(2 * L, 1), jnp.int32)
    lag = jnp.where(n < L, n, 2 * L - n)
    idx = jnp.where(n == L, 0, lag).astype(jnp.float32)
    t = jnp.where(idx == L - 1, 1.0, idx * jnp.float32(1.0 / (L - 1)))
    w = 2.0 * math.pi * idx / L
    f = jnp.linspace(1e-4, bands - 1, bands, dtype=jnp.float32)[None, :]
    flags = [(n != L).astype(jnp.float32), (n == 0).astype(jnp.float32),
             jnp.zeros((2 * L, LANES - FILTER_EMB - 2), jnp.float32)]
    return jnp.concatenate([t, jnp.cos(f * w), -jnp.sin(f * w)] + flags, axis=-1)


def _filter(L, n2, lp):
    C = HYENA_WIDTH
    tab = _filter_lag_table(L, n2)
    H = FILTER_HIDDEN
    assert 2 * H == LANES
    zeros = lambda r, c: jnp.zeros((r, c), jnp.float32)
    blockdiag = lambda w: jnp.block([[w, zeros(*w.shape)], [zeros(*w.shape), w]])
    w1p = jnp.concatenate([lp["filt_w1"], zeros(LANES - FILTER_EMB, H)], axis=0)
    w4 = lp["filt_w4"]
    w4f = jnp.concatenate([w4[:, :C], zeros(H, C)], axis=0)
    w4b = jnp.concatenate([zeros(H, C), w4[:, C:]], axis=0)
    twice = lambda v: jnp.concatenate([v, v])[None, :]
    min_decay = math.log(DECAY_TARGET) / DECAY_FAST_PCT
    max_decay = math.log(DECAY_TARGET) / DECAY_SLOW_PCT
    deltas = jnp.abs(jnp.linspace(min_decay, max_decay, C, dtype=jnp.float32))[None, :]
    full = lambda a: pl.BlockSpec(a.shape, lambda i: (0,) * a.ndim)
    args = [*_split_bf16(blockdiag(w1p)), twice(lp["filt_b1"]), *_split_bf16(blockdiag(lp["filt_w2"])),
            twice(lp["filt_b2"]), *_split_bf16(blockdiag(lp["filt_w3"])), twice(lp["filt_b3"]),
            *_split_bf16(w4f), *_split_bf16(w4b), twice(lp["filt_freq"]), deltas, lp["hyena_skip"][None, :]]
    k = pl.pallas_call(
        _filter_kernel,
        grid=(n2,),
        in_specs=[pl.BlockSpec((DFT_N1, LANES), lambda i: (i, 0))] + [full(a) for a in args],
        out_specs=pl.BlockSpec((DFT_N1, C), lambda i: (i, 0)),
        out_shape=jax.ShapeDtypeStruct((2 * L, C), jnp.bfloat16),
        compiler_params=_cparams(1),
        name="filter",
    )(tab, *args)
    return k.reshape(1, n2, DFT_N1, C)


def _dft_tables(n2):
    N = DFT_N1 * n2
    k1 = np.arange(DFT_ROWS)
    live = (k1 <= DFT_HALF).astype(np.float64)
    a = 2.0 * np.pi * ((k1[:, None] * np.arange(DFT_N1)[None, :]) % DFT_N1) / DFT_N1
    b = 2.0 * np.pi * (k1[None, :] * np.arange(n2)[:, None]) / N
    ca, sa = (jnp.asarray(t * live[:, None], jnp.float32) for t in (np.cos(a), np.sin(a)))
    cb, sb = (jnp.asarray(t, jnp.float32) for t in (np.cos(b), np.sin(b)))
    f_re = ca[None] * cb[:, :, None] - sa[None] * sb[:, :, None]
    f_im = -(sa[None] * cb[:, :, None] + ca[None] * sb[:, :, None])
    F = jnp.concatenate([f_re, f_im], axis=1).astype(jnp.bfloat16)
    weight = np.where((k1 == 0) | (k1 == DFT_HALF), 1.0, 2.0) / N
    cat, sat = (jnp.asarray((t * live[:, None] * weight[:, None]).T[:DFT_HALF], jnp.float32)
                for t in (np.cos(a), np.sin(a)))
    g_re = cat[None] * cb[:, None, :] - sat[None] * sb[:, None, :]
    g_im = -(sat[None] * cb[:, None, :] + cat[None] * sb[:, None, :])
    G = jnp.concatenate([g_re, g_im], axis=2).astype(jnp.bfloat16)
    return F, G


def _dft_mm_kernel(x_ref, m_ref, *rest, n_slabs):
    o_ref = rest[-1]
    g_ref = rest[0] if len(rest) == 2 else None
    for s in range(n_slabs):
        y = jnp.dot(m_ref[s], x_ref[s].astype(jnp.bfloat16), preferred_element_type=jnp.float32)
        if g_ref is not None:
            y = y * g_ref[s]
        o_ref[s] = y


def _dft_mm(x, mat, k_used, gate=None, tc=256, slabs=4):
    Bx, n2, K, C = x.shape
    assert K == k_used
    M = mat.shape[1]
    slabs = min(slabs, n2)
    x_spec = pl.BlockSpec((None, slabs, K, tc), lambda b, r, c: (b, r, 0, c))
    m_spec = pl.BlockSpec((slabs, M, K), lambda b, r, c: (r, 0, 0))
    o_spec = pl.BlockSpec((None, slabs, M, tc), lambda b, r, c: (b, r, 0, c))
    in_specs = [x_spec, m_spec]
    args = [x, mat]
    if gate is not None:
        in_specs.append(o_spec)
        args.append(gate)
    return pl.pallas_call(
        functools.partial(_dft_mm_kernel, n_slabs=slabs),
        grid=(Bx, n2 // slabs, C // tc),
        in_specs=in_specs,
        out_specs=o_spec,
        out_shape=jax.ShapeDtypeStruct((Bx, n2, M, C), jnp.float32),
        compiler_params=_cparams(3),
        name="dft_mm",
    )(*args)


def _fft_list(xs, sign):
    n = len(xs)
    if n == 1:
        return xs
    even = _fft_list(xs[0::2], sign)
    odd = _fft_list(xs[1::2], sign)
    out = [None] * n
    for k in range(n // 2):
        er, ei = even[k]
        pr, pi = odd[k]
        if k == 0:
            tr, ti = pr, pi
        elif 4 * k == n:
            tr, ti = (pi, -pr) if sign < 0 else (-pi, pr)
        else:
            ang = sign * 2.0 * math.pi * k / n
            wr, wi = math.cos(ang), math.sin(ang)
            tr, ti = pr * wr - pi * wi, pr * wi + pi * wr
        out[k] = (er + tr, ei + ti)
        out[k + n // 2] = (er - tr, ei - ti)
    return out


def _dft_mid_kernel(s_ref, *rest, n2, with_filter):
    o_ref = rest[-1]
    kf_ref = rest[0] if with_filter else None
    rows = s_ref.shape[2]

    def body(c, carry):
        rs = pl.ds(pl.multiple_of(c * SUBLANES, SUBLANES), SUBLANES)
        xs = [(s_ref[r, 0, rs, :], s_ref[r, 1, rs, :]) for r in range(n2)]
        ys = _fft_list(xs, -1)
        if with_filter:
            prod = []
            for k2 in range(n2):
                yr, yi = ys[k2]
                fr, fi = kf_ref[k2, 0, rs, :], kf_ref[k2, 1, rs, :]
                prod.append((yr * fr - yi * fi, yr * fi + yi * fr))
            ys = _fft_list(prod, +1)
        for r in range(n2):
            o_ref[r, 0, rs, :] = ys[r][0]
            o_ref[r, 1, rs, :] = ys[r][1]
        return carry

    lax.fori_loop(0, rows // SUBLANES, body, 0)


def _long_conv_kernel(x_ref, f_ref, g_ref, gate_ref, kf_ref, o_ref, spec, *, n2, slabs):
    nj = n2 // slabs
    j = pl.program_id(2)
    tc = spec.shape[2]

    @pl.when(j < nj)
    def _forward():
        for s in range(slabs):
            spec[j * slabs + s] = jnp.dot(f_ref[s], x_ref[s].astype(jnp.bfloat16),
                                          preferred_element_type=jnp.float32)

    @pl.when(j == nj - 1)
    def _across_slabs():
        def body(c, carry):
            re = pl.ds(pl.multiple_of(c * SUBLANES, SUBLANES), SUBLANES)
            im = pl.ds(pl.multiple_of(c * SUBLANES + DFT_ROWS, SUBLANES), SUBLANES)
            for t in range(tc // LANES):
                ls = slice(t * LANES, (t + 1) * LANES)
                ys = _fft_list([(spec[r, re, ls], spec[r, im, ls]) for r in range(n2)], -1)
                prod = []
                for k2 in range(n2):
                    yr, yi = ys[k2]
                    fr, fi = kf_ref[k2, re, ls], kf_ref[k2, im, ls]
                    prod.append((yr * fr - yi * fi, yr * fi + yi * fr))
                vs = _fft_list(prod, +1)
                for r in range(n2):
                    spec[r, re, ls] = vs[r][0]
                    spec[r, im, ls] = vs[r][1]
            return carry

        lax.fori_loop(0, DFT_ROWS // SUBLANES, body, 0)

    @pl.when(j >= nj)
    def _inverse():
        for s in range(slabs):
            v = spec[(j - nj) * slabs + s].astype(jnp.bfloat16)
            o_ref[s] = jnp.dot(g_ref[s], v, preferred_element_type=jnp.float32) * gate_ref[s]


def _long_conv(x, gate, fwd, inv, kf, tc, slabs=4):
    B, n2, K, C = x.shape
    slabs = min(slabs, n2)
    nj = n2 // slabs
    fwd_j = lambda j: jnp.minimum(j, nj - 1)
    inv_j = lambda j: jnp.maximum(j - nj, 0)
    return pl.pallas_call(
        functools.partial(_long_conv_kernel, n2=n2, slabs=slabs),
        grid=(C // tc, B, 2 * nj),
        in_specs=[pl.BlockSpec((None, slabs, K, tc), lambda c, b, j: (b, fwd_j(j), 0, c)),
                  pl.BlockSpec((slabs, DFT_M, K), lambda c, b, j: (fwd_j(j), 0, 0)),
                  pl.BlockSpec((slabs, K, DFT_M), lambda c, b, j: (inv_j(j), 0, 0)),
                  pl.BlockSpec((None, slabs, K, tc), lambda c, b, j: (b, inv_j(j), 0, c)),
                  pl.BlockSpec((n2, DFT_M, tc), lambda c, b, j: (0, 0, c), pipeline_mode=pl.Buffered(1))],
        out_specs=pl.BlockSpec((None, slabs, K, tc), lambda c, b, j: (b, inv_j(j), 0, c)),
        out_shape=jax.ShapeDtypeStruct((B, n2, K, C), jnp.float32),
        scratch_shapes=[pltpu.VMEM((n2, DFT_M, tc), jnp.float32)],
        compiler_params=pltpu.CompilerParams(dimension_semantics=("parallel", "parallel", "arbitrary"),
                                             vmem_limit_bytes=VMEM_LIMIT),
        name="long_conv",
    )(x, fwd, inv, gate, kf)


def _dft_mid(s, kf=None, tc=128, tr=88):
    Bx, n2, M, C = s.shape
    s5 = s.reshape(Bx, n2, 2, DFT_ROWS, C)
    blk = pl.BlockSpec((None, n2, 2, tr, tc), lambda c, r, b: (b, 0, 0, r, c))
    in_specs, args = [blk], [s5]
    if kf is not None:
        in_specs.append(pl.BlockSpec((n2, 2, tr, tc), lambda c, r, b: (0, 0, r, c)))
        args.append(kf.reshape(n2, 2, DFT_ROWS, C))
    out = pl.pallas_call(
        functools.partial(_dft_mid_kernel, n2=n2, with_filter=kf is not None),
        grid=(C // tc, DFT_ROWS // tr, Bx),
        in_specs=in_specs,
        out_specs=blk,
        out_shape=jax.ShapeDtypeStruct((Bx, n2, 2, DFT_ROWS, C), jnp.float32),
        compiler_params=_cparams(3),
        name="dft_mid",
    )(*args)
    return out.reshape(Bx, n2, M, C)


def _regroup_to_tokens(blk_ref, scr, dil, width):
    m = blk_ref.shape[1]
    for c in range(width // LANES):
        for r in range(dil):
            scr[c, pl.ds(r, m, stride=dil), :] = blk_ref[r, :, c * LANES:(c + 1) * LANES]


def _merge_kernel(o0_ref, l0_ref, o1_ref, l1_ref, o2_ref, l2_ref, hy_ref, za_ref, ga_ref, gh_ref,
                  wa_ref, wh_ref, out_ref, so1, so2, sl1, sl2, shy, attn_scr, hy_scr, *, n2):
    d1, d2 = ATTN_GROUPS[1][1], ATTN_GROUPS[2][1]
    _regroup_to_tokens(o1_ref, so1, d1, ATTN_WIDTH)
    _regroup_to_tokens(o2_ref, so2, d2, ATTN_WIDTH)
    _regroup_to_tokens(l1_ref, sl1, d1, LANES)
    _regroup_to_tokens(l2_ref, sl2, d2, LANES)
    _regroup_to_tokens(hy_ref, shy, n2, HYENA_WIDTH)
    l0, l1, l2 = l0_ref[...], sl1[0], sl2[0]
    mx = jnp.maximum(jnp.maximum(l0, l1), l2)
    e0, e1, e2 = jnp.exp(l0 - mx), jnp.exp(l1 - mx), jnp.exp(l2 - mx)
    inv = 1.0 / (e0 + e1 + e2)
    a0, a1, a2 = e0 * inv, e1 * inv, e2 * inv
    for h in range(N_SLOTS):
        cs = slice(h * HEAD_DIM, (h + 1) * HEAD_DIM)
        mix = a0[:, h:h + 1] * o0_ref[:, cs] + a1[:, h:h + 1] * so1[h] + a2[:, h:h + 1] * so2[h]
        z = za_ref[:, cs].astype(jnp.float32)
        attn_scr[:, cs] = (mix * (z * _sigmoid(z))).astype(jnp.bfloat16)
        hy_scr[:, cs] = shy[h].astype(jnp.bfloat16)
    br_a = jnp.dot(attn_scr[...], wa_ref[...], preferred_element_type=jnp.float32)
    br_h = jnp.dot(hy_scr[...], wh_ref[...], preferred_element_type=jnp.float32)
    merged = (_sigmoid(ga_ref[...].astype(jnp.float32)) * br_a
              + _sigmoid(gh_ref[...].astype(jnp.float32)) * br_h)
    out_ref[...] = merged.astype(out_ref.dtype)


def _merge(outs, lses, hyena, gates, w_ba, w_bh, n2):
    B, d0, L, W = outs[0].shape
    D = w_ba.shape[1]
    tm = MERGE_TM
    d1, d2 = ATTN_GROUPS[1][1], ATTN_GROUPS[2][1]

    def cls(d, width):
        return pl.BlockSpec((None, d, tm // d, width), lambda b, i: (b, 0, i, 0))

    in_specs = [pl.BlockSpec((None, None, tm, W), lambda b, i: (b, 0, i, 0)),
                pl.BlockSpec((None, None, tm, LANES), lambda b, i: (b, 0, i, 0)),
                cls(d1, W), cls(d1, LANES), cls(d2, W), cls(d2, LANES), cls(n2, HYENA_WIDTH),
                pl.BlockSpec((None, tm, W), lambda b, i: (b, i, 0)),
                pl.BlockSpec((None, tm, D), lambda b, i: (b, i, 1)),
                pl.BlockSpec((None, tm, D), lambda b, i: (b, i, 2)),
                pl.BlockSpec(w_ba.shape, lambda b, i: (0, 0), pipeline_mode=pl.Buffered(1)),
                pl.BlockSpec(w_bh.shape, lambda b, i: (0, 0), pipeline_mode=pl.Buffered(1))]
    wt = W // LANES
    return pl.pallas_call(
        functools.partial(_merge_kernel, n2=n2),
        grid=(B, L // tm),
        in_specs=in_specs,
        out_specs=pl.BlockSpec((None, tm, D), lambda b, i: (b, i, 0)),
        out_shape=jax.ShapeDtypeStruct((B, L, D), jnp.bfloat16),
        scratch_shapes=[pltpu.VMEM((wt, tm, LANES), jnp.float32), pltpu.VMEM((wt, tm, LANES), jnp.float32),
                        pltpu.VMEM((1, tm, LANES), jnp.float32), pltpu.VMEM((1, tm, LANES), jnp.float32),
                        pltpu.VMEM((wt, tm, LANES), jnp.float32),
                        pltpu.VMEM((tm, W), jnp.bfloat16), pltpu.VMEM((tm, HYENA_WIDTH), jnp.bfloat16)],
        compiler_params=_cparams(2),
        name="merge",
    )(outs[0], lses[0], outs[1], lses[1], outs[2], lses[2], hyena, gates, gates, gates, w_ba, w_bh)


def _out_kernel(m_ref, w_ref, x_ref, g_ref, o_ref, *, final_norm):
    y = x_ref[...] + jnp.dot(m_ref[...], w_ref[...], preferred_element_type=jnp.float32)
    if final_norm:
        ms = jnp.mean(y * y, axis=-1, keepdims=True)
        y = y * lax.rsqrt(ms + RMS_EPS) * g_ref[...]
    o_ref[...] = y


def _out_proj(merged, w_out, x, final_gain, final_norm):
    B, L, D = x.shape
    tm = MERGE_TM
    tok = pl.BlockSpec((None, tm, D), lambda b, i: (b, i, 0))
    return pl.pallas_call(
        functools.partial(_out_kernel, final_norm=final_norm),
        grid=(B, L // tm),
        in_specs=[tok, pl.BlockSpec(w_out.shape, lambda b, i: (0, 0), pipeline_mode=pl.Buffered(1)),
                  tok, pl.BlockSpec((1, D), lambda b, i: (0, 0))],
        out_specs=tok,
        out_shape=jax.ShapeDtypeStruct((B, L, D), jnp.float32),
        compiler_params=_cparams(2),
        name="out_proj",
    )(merged, w_out, x, final_gain)


def _prep_layer(i, norm_gain, w_in, conv_w, conv_b, filt_w1, filt_b1, filt_w2, filt_b2, filt_w3, filt_b3,
                filt_w4, filt_freq, hyena_skip, w_branch_attn, w_branch_hyena, w_out):
    bf = jnp.bfloat16
    return dict(
        gain=norm_gain[i][None, :],
        w_in=w_in[i].astype(bf),
        conv_w=conv_w[i], conv_b=conv_b[i][None, :],
        filt_w1=filt_w1[i], filt_b1=filt_b1[i], filt_w2=filt_w2[i], filt_b2=filt_b2[i],
        filt_w3=filt_w3[i], filt_b3=filt_b3[i], filt_w4=filt_w4[i], filt_freq=filt_freq[i],
        hyena_skip=hyena_skip[i],
        w_ba=w_branch_attn[i].astype(bf), w_bh=w_branch_hyena[i].astype(bf), w_out=w_out[i].astype(bf))


def _layer(x, lp, final_gain, final_norm):
    B, L, D = x.shape
    assert L % (2 * DFT_HALF) == 0 and L % PROJ_TM == 0 and D == 2 * W_BLOCK
    n2 = L // DFT_HALF

    tabs = _rope_tables(L)
    outs, lses = [], []
    for g, (_, dil) in enumerate(ATTN_GROUPS):
        qkv = _proj_qkv(x, lp["gain"], lp["w_in"], tabs, dil, g)
        n = L // dil
        o, lse = _attention(qkv.reshape(3, B * dil, n, ATTN_WIDTH))
        outs.append(o.reshape(B, dil, n, ATTN_WIDTH))
        lses.append(lse.reshape(B, dil, n, LANES))

    gate_cols = (lambda s: jnp.where(s == 0, GATE_BLOCKS[0], GATE_BLOCKS[2] + 2 * (s - 1)),
                 lambda s: jnp.where(s == 0, GATE_BLOCKS[1], GATE_BLOCKS[2] + 2 * (s - 1) + 1))
    gates = _proj_plain(x, lp["gain"], lp["w_in"], gate_cols, 3, jnp.bfloat16)
    uhy_cols = tuple(functools.partial(lambda s, c: c, c=UHY_BLOCK + k) for k in range(3))
    u_hy = _proj_plain(x, lp["gain"], lp["w_in"], uhy_cols, 1, jnp.float32)
    gated, mult = _hyena_pre(u_hy, gates, lp["conv_w"], lp["conv_b"], n2)
    fwd_tabs, inv_tabs = _dft_tables(n2)
    kf = _dft_mid(_dft_mm(_filter(L, n2, lp), fwd_tabs, DFT_N1))[0]
    tc = 2 * LANES if 2 * n2 * DFT_M * 2 * LANES * 4 <= LONG_CONV_VMEM else LANES
    hyena = _long_conv(gated, mult, fwd_tabs, inv_tabs, kf, tc)

    merged = _merge(outs, lses, hyena, gates, lp["w_ba"], lp["w_bh"], n2)
    return _out_proj(merged, lp["w_out"], x, final_gain, final_norm)


def kernel(x_prompt, x_sample, norm_gain, w_in, conv_w, conv_b, filt_w1, filt_b1, filt_w2, filt_b2, filt_w3, filt_b3, filt_w4, filt_freq, hyena_skip, w_branch_attn, w_branch_hyena, w_out, final_gain):
    depth = w_in.shape[0]
    layers = [_prep_layer(i, norm_gain, w_in, conv_w, conv_b, filt_w1, filt_b1, filt_w2, filt_b2, filt_w3,
                          filt_b3, filt_w4, filt_freq, hyena_skip, w_branch_attn, w_branch_hyena, w_out)
              for i in range(depth)]
    fg = final_gain[None, :]

    def trunk(x):
        for i, lp in enumerate(layers):
            x = _layer(x, lp, fg, final_norm=(i == depth - 1))
        return x

    return (trunk(x_prompt), trunk(x_sample))
```

```python
import functools
import math

import jax
import jax.numpy as jnp
import numpy as np
from jax import lax
from jax.experimental import pallas as pl
from jax.experimental.pallas import tpu as pltpu

HEAD_DIM = 128
N_SLOTS = 8
ATTN_GROUPS = ((128, 1), (512, 4), (2048, 16))
N_GROUPS = len(ATTN_GROUPS)
ATTN_WIDTH = N_SLOTS * HEAD_DIM
QKV_WIDTH = N_GROUPS * ATTN_WIDTH
ROT_DIM = HEAD_DIM // 4
ROPE_THETA = 500000.0
HYENA_WIDTH = 1024
FILTER_EMB = 33
FILTER_HIDDEN = 64
DECAY_TARGET = 1e-2
DECAY_FAST_PCT = 0.3
DECAY_SLOW_PCT = 1.5
RMS_EPS = 1e-6
RADIUS = 64

LANES = 128
SUBLANES = 8
VMEM_LIMIT = 56 * 1024 * 1024
LONG_CONV_VMEM = 36 * 1024 * 1024
LONG_CONV_RESIDENT_VMEM = 48 * 1024 * 1024

DFT_N1 = 512
DFT_HALF = DFT_N1 // 2
DFT_ROWS = 264
DFT_M = 2 * DFT_ROWS

PROJ_TM = 512
W_BLOCK = 1024
_ZA_BLOCK = 3 * QKV_WIDTH // W_BLOCK
UHY_BLOCK = _ZA_BLOCK + ATTN_WIDTH // W_BLOCK
_ZH_BLOCK = UHY_BLOCK + 3 * HYENA_WIDTH // W_BLOCK
GATE_BLOCKS = (_ZA_BLOCK, _ZH_BLOCK, _ZH_BLOCK + HYENA_WIDTH // W_BLOCK)
PERM_MIN_DIL = 8
ATTN_TQ = 512
ATTN_SUB = 128
HY_TL = 512
HALO_ROWS = 16
MERGE_TM = 512


def _cparams(n_axes):
    return pltpu.CompilerParams(dimension_semantics=("parallel",) * n_axes,
                                vmem_limit_bytes=VMEM_LIMIT)


def _sigmoid(x):
    return 0.5 * jnp.tanh(0.5 * x) + 0.5


def _split_bf16(x):
    hi = x.astype(jnp.bfloat16)
    lo = (x - hi.astype(jnp.float32)).astype(jnp.bfloat16)
    return hi, lo


def _normed(x_ref, g_ref):
    x = x_ref[...]
    ms = jnp.mean(x * x, axis=-1, keepdims=True)
    return (x * lax.rsqrt(ms + RMS_EPS) * g_ref[...]).astype(jnp.bfloat16)


def _proj_plain_kernel(x_ref, g_ref, *rest):
    w_refs, o_ref = rest[:-1], rest[-1]
    h = _normed(x_ref, g_ref)
    for k, w_ref in enumerate(w_refs):
        cw = w_ref.shape[1]
        o_ref[:, k * cw:(k + 1) * cw] = jnp.dot(h, w_ref[...], preferred_element_type=jnp.float32).astype(o_ref.dtype)


def _proj_plain(x, gain, wb, col_maps, n_slabs, out_dtype):
    B, L, D = x.shape
    tm = PROJ_TM
    cs = len(col_maps) * W_BLOCK
    w_specs = [pl.BlockSpec((D, W_BLOCK), functools.partial(lambda s, b, i, f: (0, f(s)), f=f),
                            pipeline_mode=pl.Buffered(1)) for f in col_maps]
    return pl.pallas_call(
        _proj_plain_kernel,
        grid=(n_slabs, B, L // tm),
        in_specs=[pl.BlockSpec((None, tm, D), lambda s, b, i: (b, i, 0)),
                  pl.BlockSpec((1, D), lambda s, b, i: (0, 0))] + w_specs,
        out_specs=pl.BlockSpec((None, tm, cs), lambda s, b, i: (b, i, s)),
        out_shape=jax.ShapeDtypeStruct((B, L, n_slabs * cs), out_dtype),
        compiler_params=_cparams(3),
        name="proj_plain",
    )(x, gain, *([wb] * len(col_maps)))


def _proj_qkv_kernel(x_ref, g_ref, wq_ref, wk_ref, wv_ref, cq_ref, sq_ref, ck_ref, sk_ref, *rest, dil, perm):
    tm = x_ref.shape[0]
    rows = tm // dil
    h = _normed(x_ref, g_ref)
    if perm:
        p_ref, o_ref = rest
        h = jnp.dot(p_ref[...], h, preferred_element_type=jnp.float32).astype(jnp.bfloat16)
    elif dil > 1:
        o_ref, scr = rest
    else:
        (o_ref,) = rest
    lane = lax.broadcasted_iota(jnp.int32, (tm, HEAD_DIM), 1)
    first_half = lane < (ROT_DIM // 2)
    for which, w_ref in enumerate((wq_ref, wk_ref, wv_ref)):
        acc = jnp.dot(h, w_ref[...], preferred_element_type=jnp.float32)
        for slot in range(N_SLOTS):
            cs = slice(slot * HEAD_DIM, (slot + 1) * HEAD_DIM)
            t = acc[:, cs]
            if which < 2:
                cos_ref, sin_ref = (cq_ref, sq_ref) if which == 0 else (ck_ref, sk_ref)
                partner = jnp.where(first_half,
                                    pltpu.roll(t, HEAD_DIM - ROT_DIM // 2, axis=1),
                                    pltpu.roll(t, ROT_DIM // 2, axis=1))
                t = t * cos_ref[...] + partner * sin_ref[...]
            if perm or dil == 1:
                for r in range(dil):
                    o_ref[which, r, :, cs] = t[r * rows:(r + 1) * rows].astype(o_ref.dtype)
            else:
                scr[slot] = t
        if not perm and dil > 1:
            for slot in range(N_SLOTS):
                for r in range(dil):
                    o_ref[which, r, :, slot * HEAD_DIM:(slot + 1) * HEAD_DIM] = (
                        scr[slot, pl.ds(r, rows, stride=dil), :].astype(o_ref.dtype))


def _class_major(a, tm, dil):
    n_tiles = a.shape[0] // tm
    return a.reshape(n_tiles, tm // dil, dil, -1).transpose(0, 2, 1, 3).reshape(a.shape)


def _class_major_onehot(tm, dil):
    src = np.arange(tm).reshape(tm // dil, dil).T.reshape(tm)
    onehot = np.zeros((tm, tm), np.float32)
    onehot[np.arange(tm), src] = 1.0
    return jnp.asarray(onehot, jnp.bfloat16)


def _proj_qkv(x, gain, wb, tabs, dil, group):
    B, L, D = x.shape
    tm = PROJ_TM
    n = L // dil
    perm = dil >= PERM_MIN_DIL
    tab_spec = pl.BlockSpec((tm, HEAD_DIM), lambda b, i: (i, 0))
    w_specs = [pl.BlockSpec((D, W_BLOCK), functools.partial(lambda b, i, c: (0, c), c=t * N_GROUPS + group),
                            pipeline_mode=pl.Buffered(1)) for t in range(3)]
    in_specs = [pl.BlockSpec((None, tm, D), lambda b, i: (b, i, 0)),
                pl.BlockSpec((1, D), lambda b, i: (0, 0))] + w_specs + [tab_spec] * 4
    args = [x, gain, wb, wb, wb]
    scratch = []
    if perm:
        args += [_class_major(t, tm, dil) for t in tabs] + [_class_major_onehot(tm, dil)]
        in_specs.append(pl.BlockSpec((tm, tm), lambda b, i: (0, 0), pipeline_mode=pl.Buffered(1)))
    else:
        args += list(tabs)
        if dil > 1:
            scratch = [pltpu.VMEM((N_SLOTS, tm, HEAD_DIM), jnp.float32)]
    return pl.pallas_call(
        functools.partial(_proj_qkv_kernel, dil=dil, perm=perm),
        grid=(B, L // tm),
        in_specs=in_specs,
        out_specs=pl.BlockSpec((3, None, dil, tm // dil, ATTN_WIDTH), lambda b, i: (0, b, 0, i, 0)),
        out_shape=jax.ShapeDtypeStruct((3, B, dil, n, ATTN_WIDTH), jnp.bfloat16),
        scratch_shapes=scratch,
        compiler_params=_cparams(2),
        name=f"proj_qkv_d{dil}",
    )(*args)


def _rope_tables(L):
    inv_freq = jnp.power(ROPE_THETA, -jnp.arange(0, ROT_DIM, 2, dtype=jnp.float32) / ROT_DIM)
    ang = jnp.arange(L, dtype=jnp.float32)[:, None] * inv_freq[None, :]
    cos, sin = jnp.cos(ang), jnp.sin(ang)
    ones = jnp.ones((L, HEAD_DIM - ROT_DIM), jnp.float32)
    cos_t = jnp.concatenate([cos, cos, ones], axis=1)
    sin_t = jnp.concatenate([-sin, sin, 0.0 * ones], axis=1)
    scale = 1.0 / math.sqrt(HEAD_DIM)
    return cos_t * scale, sin_t * scale, cos_t, sin_t


def _attn_kernel(q_ref, kp_ref, kc_ref, kn_ref, vp_ref, vc_ref, vn_ref, o_ref, lse_ref, kw, vw, *, n):
    tq = q_ref.shape[0]
    i = pl.program_id(1)
    kw[0:RADIUS] = kp_ref[...]
    kw[RADIUS:RADIUS + tq] = kc_ref[...]
    kw[RADIUS + tq:] = kn_ref[...]
    vw[0:RADIUS] = vp_ref[...]
    vw[RADIUS:RADIUS + tq] = vc_ref[...]
    vw[RADIUS + tq:] = vn_ref[...]
    win = ATTN_SUB + 2 * RADIUS
    row = lax.broadcasted_iota(jnp.int32, (ATTN_SUB, win), 0)
    col = lax.broadcasted_iota(jnp.int32, (ATTN_SUB, win), 1)
    band = (col >= row) & (col <= row + 2 * RADIUS)
    lane = lax.broadcasted_iota(jnp.int32, (ATTN_SUB, LANES), 1)
    nsub = tq // ATTN_SUB
    for s in range(nsub):
        q0 = i * tq + s * ATTN_SUB
        valid = band
        if s == 0:
            valid = valid & (col >= RADIUS - q0)
        if s == nsub - 1:
            valid = valid & (col < n - q0 + RADIUS)
        lse_tile = jnp.zeros((ATTN_SUB, LANES), jnp.float32)
        for h in range(N_SLOTS):
            cs = slice(h * HEAD_DIM, (h + 1) * HEAD_DIM)
            qb = q_ref[s * ATTN_SUB:(s + 1) * ATTN_SUB, cs]
            kb = kw[s * ATTN_SUB:s * ATTN_SUB + win, cs]
            vb = vw[s * ATTN_SUB:s * ATTN_SUB + win, cs]
            sc = lax.dot_general(qb, kb, (((1,), (1,)), ((), ())), preferred_element_type=jnp.float32)
            sc = jnp.where(valid, sc, -1e30)
            m = jnp.max(sc, axis=1, keepdims=True)
            p = jnp.exp(sc - m)
            l = jnp.sum(p, axis=1, keepdims=True)
            o = jnp.dot(p.astype(jnp.bfloat16), vb, preferred_element_type=jnp.float32)
            o_ref[s * ATTN_SUB:(s + 1) * ATTN_SUB, cs] = o / l
            lse_tile = jnp.where(lane == h, m + jnp.log(l), lse_tile)
        lse_ref[s * ATTN_SUB:(s + 1) * ATTN_SUB, :] = lse_tile


def _attention(qkv):
    _, S, n, W = qkv.shape
    tq = min(ATTN_TQ, n)
    r = tq // RADIUS
    last = n // RADIUS - 1

    def cur(which):
        return pl.BlockSpec((None, None, tq, W), lambda s, i: (which, s, i, 0))

    def prev(which):
        return pl.BlockSpec((None, None, RADIUS, W), lambda s, i: (which, s, jnp.maximum(i * r - 1, 0), 0))

    def nxt(which):
        return pl.BlockSpec((None, None, RADIUS, W), lambda s, i: (which, s, jnp.minimum((i + 1) * r, last), 0))

    return pl.pallas_call(
        functools.partial(_attn_kernel, n=n),
        grid=(S, n // tq),
        in_specs=[cur(0), prev(1), cur(1), nxt(1), prev(2), cur(2), nxt(2)],
        out_specs=[pl.BlockSpec((None, tq, W), lambda s, i: (s, i, 0)),
                   pl.BlockSpec((None, tq, LANES), lambda s, i: (s, i, 0))],
        out_shape=[jax.ShapeDtypeStruct((S, n, W), jnp.float32),
                   jax.ShapeDtypeStruct((S, n, LANES), jnp.float32)],
        scratch_shapes=[pltpu.VMEM((tq + 2 * RADIUS, W), jnp.bfloat16),
                        pltpu.VMEM((tq + 2 * RADIUS, W), jnp.bfloat16)],
        compiler_params=_cparams(2),
        name="attn",
    )(qkv, qkv, qkv, qkv, qkv, qkv, qkv)


def _shift_rows(u, edge_row, down):
    tl = u.shape[0]
    rolled = pltpu.roll(u, 1 if down else tl - 1, axis=0)
    r8 = lax.broadcasted_iota(jnp.int32, (SUBLANES, u.shape[1]), 0)
    if down:
        head = jnp.where(r8 == 0, edge_row, rolled[:SUBLANES])
        return jnp.concatenate([head, rolled[SUBLANES:]], axis=0)
    tail = jnp.where(r8 == SUBLANES - 1, edge_row, rolled[tl - SUBLANES:])
    return jnp.concatenate([rolled[:tl - SUBLANES], tail], axis=0)


def _hyena_pre_kernel(u_ref, up_ref, un_ref, z_ref, cw_ref, cb_ref, p_ref, g_ref, m_ref, *, n2, nblk):
    tl = u_ref.shape[0]
    i = pl.program_id(1)
    has_prev = (i > 0).astype(jnp.float32)
    has_next = (i < nblk - 1).astype(jnp.float32)
    C = HYENA_WIDTH
    rows = tl // n2
    perm = p_ref[...]

    def conv(cs):
        u = u_ref[:, cs].astype(jnp.float32)
        edge_prev = up_ref[HALO_ROWS - 1:HALO_ROWS, cs].astype(jnp.float32) * has_prev
        edge_next = un_ref[0:1, cs].astype(jnp.float32) * has_next
        u_prev = _shift_rows(u, edge_prev, True)
        u_next = _shift_rows(u, edge_next, False)
        return u_prev * cw_ref[0:1, cs] + u * cw_ref[1:2, cs] + u_next * cw_ref[2:3, cs] + cb_ref[:, cs]

    for c in range(C // LANES):
        ls = slice(c * LANES, (c + 1) * LANES)
        x0, x1, vh = (conv(slice(k * C + c * LANES, k * C + (c + 1) * LANES)) for k in range(3))
        z = z_ref[:, ls].astype(jnp.float32)
        m_hi, m_lo = _split_bf16(x0 * (z * _sigmoid(z)))
        stack = jnp.concatenate([(vh * x1).astype(jnp.bfloat16), m_hi, m_lo], axis=1)
        res = jnp.dot(perm, stack, preferred_element_type=jnp.float32)
        g = res[:, :LANES]
        m = res[:, LANES:2 * LANES] + res[:, 2 * LANES:]
        for r in range(n2):
            g_ref[r, :, ls] = g[r * rows:(r + 1) * rows]
            m_ref[r, :, ls] = m[r * rows:(r + 1) * rows]


def _hyena_pre(u_hy, gates, conv_w, conv_b, n2):
    B, L, C3 = u_hy.shape
    C = HYENA_WIDTH
    tl = HY_TL
    nblk = L // tl
    per = tl // HALO_ROWS
    last_halo = L // HALO_ROWS - 1
    out_spec = pl.BlockSpec((None, n2, tl // n2, C), lambda b, i: (b, 0, i, 0))
    out_sds = jax.ShapeDtypeStruct((B, n2, L // n2, C), jnp.float32)
    return pl.pallas_call(
        functools.partial(_hyena_pre_kernel, n2=n2, nblk=nblk),
        grid=(B, nblk),
        in_specs=[pl.BlockSpec((None, tl, C3), lambda b, i: (b, i, 0)),
                  pl.BlockSpec((None, HALO_ROWS, C3), lambda b, i: (b, jnp.maximum(i * per - 1, 0), 0)),
                  pl.BlockSpec((None, HALO_ROWS, C3), lambda b, i: (b, jnp.minimum((i + 1) * per, last_halo), 0)),
                  pl.BlockSpec((None, tl, C), lambda b, i: (b, i, 1)),
                  pl.BlockSpec((3, C3), lambda b, i: (0, 0)),
                  pl.BlockSpec((1, C3), lambda b, i: (0, 0)),
                  pl.BlockSpec((tl, tl), lambda b, i: (0, 0), pipeline_mode=pl.Buffered(1))],
        out_specs=[out_spec, out_spec],
        out_shape=[out_sds, out_sds],
        compiler_params=_cparams(2),
        name="hyena_pre",
    )(u_hy, u_hy, u_hy, gates, conv_w, conv_b, _class_major_onehot(tl, n2))


POS_VALID, POS_ZERO = FILTER_EMB, FILTER_EMB + 1


def _dot3(a, wh_ref, wl_ref):
    ah, al = _split_bf16(a)
    wh = wh_ref[...]
    return (jnp.dot(ah, wh, preferred_element_type=jnp.float32)
            + jnp.dot(al, wh, preferred_element_type=jnp.float32)
            + jnp.dot(ah, wl_ref[...], preferred_element_type=jnp.float32))


def _filter_kernel(p_ref, w1h, w1l, b1_ref, w2h, w2l, b2_ref, w3h, w3l, b3_ref, w4fh, w4fl, w4bh, w4bl,
                   fr_ref, dl_ref, sk_ref, o_ref):
    p = p_ref[...]
    fr = fr_ref[...]
    p2 = jnp.concatenate([p[:DFT_HALF], p[DFT_HALF:]], axis=1)
    hdn = jnp.sin(fr * (_dot3(p2, w1h, w1l) + b1_ref[...]))
    hdn = jnp.sin(fr * (_dot3(hdn, w2h, w2l) + b2_ref[...]))
    hdn = jnp.sin(fr * (_dot3(hdn, w3h, w3l) + b3_ref[...]))
    scale = jnp.exp(-p[:, 0:1] * dl_ref[...]) * p[:, POS_VALID:POS_VALID + 1]
    skip = p[:, POS_ZERO:POS_ZERO + 1] * sk_ref[...]
    for d, (wh, wl) in enumerate(((w4fh, w4fl), (w4bh, w4bl))):
        rs = slice(d * DFT_HALF, (d + 1) * DFT_HALF)
        o_ref[rs, :] = (_dot3(hdn, wh, wl) * scale[rs] + skip[rs]).astype(o_ref.dtype)


def _filter_lag_table(L, n2):
    bands = (FILTER_EMB - 1) // 2
    n = jnp.asarray(np.arange(2 * L).reshape(2 * L // n2, n2).T.reshape(1, 2 * L), jnp.int32)
    lag = jnp.where(n < L, n, 2 * L - n)
    idx = jnp.where(n == L, 0, lag).astype(jnp.float32)
    t = jnp.where(idx == L - 1, 1.0, idx * jnp.float32(1.0 / (L - 1)))
    w = 2.0 * math.pi * idx / L
    f = jnp.linspace(1e-4, bands - 1, bands, dtype=jnp.float32)[:, None]
    flags = [(n != L).astype(jnp.float32), (n == 0).astype(jnp.float32),
             jnp.zeros((LANES - FILTER_EMB - 2, 2 * L), jnp.float32)]
    return jnp.concatenate([t, jnp.cos(f * w), -jnp.sin(f * w)] + flags, axis=0).T


def _filter(L, n2, lp):
    C = HYENA_WIDTH
    tab = _filter_lag_table(L, n2)
    H = FILTER_HIDDEN
    assert 2 * H == LANES
    zeros = lambda r, c: jnp.zeros((r, c), jnp.float32)
    blockdiag = lambda w: jnp.block([[w, zeros(*w.shape)], [zeros(*w.shape), w]])
    w1p = jnp.concatenate([lp["filt_w1"], zeros(LANES - FILTER_EMB, H)], axis=0)
    w4 = lp["filt_w4"]
    w4f = jnp.concatenate([w4[:, :C], zeros(H, C)], axis=0)
    w4b = jnp.concatenate([zeros(H, C), w4[:, C:]], axis=0)
    twice = lambda v: jnp.concatenate([v, v])[None, :]
    min_decay = math.log(DECAY_TARGET) / DECAY_FAST_PCT
    max_decay = math.log(DECAY_TARGET) / DECAY_SLOW_PCT
    deltas = jnp.abs(jnp.linspace(min_decay, max_decay, C, dtype=jnp.float32))[None, :]
    full = lambda a: pl.BlockSpec(a.shape, lambda i: (0,) * a.ndim)
    args = [*_split_bf16(blockdiag(w1p)), twice(lp["filt_b1"]), *_split_bf16(blockdiag(lp["filt_w2"])),
            twice(lp["filt_b2"]), *_split_bf16(blockdiag(lp["filt_w3"])), twice(lp["filt_b3"]),
            *_split_bf16(w4f), *_split_bf16(w4b), twice(lp["filt_freq"]), deltas, lp["hyena_skip"][None, :]]
    k = pl.pallas_call(
        _filter_kernel,
        grid=(n2,),
        in_specs=[pl.BlockSpec((DFT_N1, LANES), lambda i: (i, 0))] + [full(a) for a in args],
        out_specs=pl.BlockSpec((DFT_N1, C), lambda i: (i, 0)),
        out_shape=jax.ShapeDtypeStruct((2 * L, C), jnp.bfloat16),
        compiler_params=_cparams(1),
        name="filter",
    )(tab, *args)
    return k.reshape(n2, DFT_N1, C)


def _dft_tables(n2):
    N = DFT_N1 * n2
    k1 = np.arange(DFT_ROWS)
    live = (k1 <= DFT_HALF).astype(np.float64)
    a = 2.0 * np.pi * ((k1[:, None] * np.arange(DFT_N1)[None, :]) % DFT_N1) / DFT_N1
    b = 2.0 * np.pi * (k1[None, :] * np.arange(n2)[:, None]) / N
    ca, sa = (jnp.asarray(t * live[:, None], jnp.float32) for t in (np.cos(a), np.sin(a)))
    cb, sb = (jnp.asarray(t, jnp.float32) for t in (np.cos(b), np.sin(b)))
    f_re = ca[None] * cb[:, :, None] - sa[None] * sb[:, :, None]
    f_im = -(sa[None] * cb[:, :, None] + ca[None] * sb[:, :, None])
    F = jnp.concatenate([f_re, f_im], axis=1).astype(jnp.bfloat16)
    weight = np.where((k1 == 0) | (k1 == DFT_HALF), 1.0, 2.0) / N
    cat, sat = (jnp.asarray((t * live[:, None] * weight[:, None]).T[:DFT_HALF], jnp.float32)
                for t in (np.cos(a), np.sin(a)))
    g_re = cat[None] * cb[:, None, :] - sat[None] * sb[:, None, :]
    g_im = -(sat[None] * cb[:, None, :] + cat[None] * sb[:, None, :])
    G = jnp.concatenate([g_re, g_im], axis=2).astype(jnp.bfloat16)
    return F, G


def _filter_spectrum_kernel(k_ref, f_ref, o_ref, *, n2, slabs):
    nj = n2 // slabs
    j = pl.program_id(1)
    for s in range(slabs):
        o_ref[j * slabs + s] = jnp.dot(f_ref[s], k_ref[s], preferred_element_type=jnp.float32)

    @pl.when(j == nj - 1)
    def _across_slabs():
        def body(c, carry):
            re = pl.ds(pl.multiple_of(c * SUBLANES, SUBLANES), SUBLANES)
            im = pl.ds(pl.multiple_of(c * SUBLANES + DFT_ROWS, SUBLANES), SUBLANES)
            ys = _fft_list([(o_ref[r, re, :], o_ref[r, im, :]) for r in range(n2)], -1)
            for k2 in range(n2):
                o_ref[k2, re, :] = ys[k2][0]
                o_ref[k2, im, :] = ys[k2][1]
            return carry

        lax.fori_loop(0, DFT_ROWS // SUBLANES, body, 0)


def _filter_spectrum(k, fwd, slabs=8):
    n2, K, C = k.shape
    slabs = min(slabs, n2)
    tc = LANES
    return pl.pallas_call(
        functools.partial(_filter_spectrum_kernel, n2=n2, slabs=slabs),
        grid=(C // tc, n2 // slabs),
        in_specs=[pl.BlockSpec((slabs, K, tc), lambda c, j: (j, 0, c)),
                  pl.BlockSpec((slabs, DFT_M, K), lambda c, j: (j, 0, 0))],
        out_specs=pl.BlockSpec((n2, DFT_M, tc), lambda c, j: (0, 0, c)),
        out_shape=jax.ShapeDtypeStruct((n2, DFT_M, C), jnp.float32),
        compiler_params=pltpu.CompilerParams(dimension_semantics=("parallel", "arbitrary"),
                                             vmem_limit_bytes=VMEM_LIMIT),
        name="filter_spectrum",
    )(k, fwd)


def _fft_list(xs, sign):
    n = len(xs)
    if n == 1:
        return xs
    even = _fft_list(xs[0::2], sign)
    odd = _fft_list(xs[1::2], sign)
    out = [None] * n
    for k in range(n // 2):
        er, ei = even[k]
        pr, pi = odd[k]
        if k == 0:
            tr, ti = pr, pi
        elif 4 * k == n:
            tr, ti = (pi, -pr) if sign < 0 else (-pi, pr)
        else:
            ang = sign * 2.0 * math.pi * k / n
            wr, wi = math.cos(ang), math.sin(ang)
            tr, ti = pr * wr - pi * wi, pr * wi + pi * wr
        out[k] = (er + tr, ei + ti)
        out[k + n // 2] = (er - tr, ei - ti)
    return out


def _long_conv_kernel(x_ref, f_ref, g_ref, gate_ref, kf_ref, o_ref, spec, *, n2, slabs, resident):
    nj = n2 // slabs
    j = pl.program_id(2)
    tc = spec.shape[2]
    fwd_base = j * slabs if resident else 0
    inv_base = (j - nj) * slabs if resident else 0

    @pl.when(j < nj)
    def _forward():
        for s in range(slabs):
            spec[j * slabs + s] = jnp.dot(f_ref[fwd_base + s], x_ref[s].astype(jnp.bfloat16),
                                          preferred_element_type=jnp.float32)

    @pl.when(j == nj - 1)
    def _across_slabs():
        def body(c, carry):
            re = pl.ds(pl.multiple_of(c * SUBLANES, SUBLANES), SUBLANES)
            im = pl.ds(pl.multiple_of(c * SUBLANES + DFT_ROWS, SUBLANES), SUBLANES)
            for t in range(tc // LANES):
                ls = slice(t * LANES, (t + 1) * LANES)
                ys = _fft_list([(spec[r, re, ls], spec[r, im, ls]) for r in range(n2)], -1)
                prod = []
                for k2 in range(n2):
                    yr, yi = ys[k2]
                    fr, fi = kf_ref[k2, re, ls], kf_ref[k2, im, ls]
                    prod.append((yr * fr - yi * fi, yr * fi + yi * fr))
                vs = _fft_list(prod, +1)
                for r in range(n2):
                    spec[r, re, ls] = vs[r][0]
                    spec[r, im, ls] = vs[r][1]
            return carry

        lax.fori_loop(0, DFT_ROWS // SUBLANES, body, 0)

    @pl.when(j >= nj)
    def _inverse():
        for s in range(slabs):
            v = spec[(j - nj) * slabs + s].astype(jnp.bfloat16)
            o_ref[inv_base + s] = (jnp.dot(g_ref[inv_base + s], v, preferred_element_type=jnp.float32)
                                   * gate_ref[inv_base + s])


def _long_conv(x, gate, fwd, inv, kf, slabs=8):
    B, n2, K, C = x.shape
    slabs = min(slabs, n2)
    nj = n2 // slabs
    spectra = lambda tc: 2 * n2 * DFT_M * tc * 4
    tc = 2 * LANES if spectra(2 * LANES) <= LONG_CONV_VMEM else LANES
    whole = 2 * n2 * DFT_M * K * 2 + 2 * 2 * n2 * K * tc * 4
    resident = spectra(tc) + whole + 2 * slabs * K * tc * 4 <= LONG_CONV_RESIDENT_VMEM
    fwd_j = lambda j: jnp.minimum(j, nj - 1)
    inv_j = lambda j: jnp.maximum(j - nj, 0)
    if resident:
        f_spec = pl.BlockSpec((n2, DFT_M, K), lambda c, b, j: (0, 0, 0), pipeline_mode=pl.Buffered(1))
        g_spec = pl.BlockSpec((n2, K, DFT_M), lambda c, b, j: (0, 0, 0), pipeline_mode=pl.Buffered(1))
        io_spec = pl.BlockSpec((None, n2, K, tc), lambda c, b, j: (b, 0, 0, c))
    else:
        f_spec = pl.BlockSpec((slabs, DFT_M, K), lambda c, b, j: (fwd_j(j), 0, 0))
        g_spec = pl.BlockSpec((slabs, K, DFT_M), lambda c, b, j: (inv_j(j), 0, 0))
        io_spec = pl.BlockSpec((None, slabs, K, tc), lambda c, b, j: (b, inv_j(j), 0, c))
    return pl.pallas_call(
        functools.partial(_long_conv_kernel, n2=n2, slabs=slabs, resident=resident),
        grid=(C // tc, B, 2 * nj),
        in_specs=[pl.BlockSpec((None, slabs, K, tc), lambda c, b, j: (b, fwd_j(j), 0, c)),
                  f_spec, g_spec, io_spec,
                  pl.BlockSpec((n2, DFT_M, tc), lambda c, b, j: (0, 0, c), pipeline_mode=pl.Buffered(1))],
        out_specs=io_spec,
        out_shape=jax.ShapeDtypeStruct((B, n2, K, C), jnp.float32),
        scratch_shapes=[pltpu.VMEM((n2, DFT_M, tc), jnp.float32)],
        compiler_params=pltpu.CompilerParams(dimension_semantics=("parallel", "parallel", "arbitrary"),
                                             vmem_limit_bytes=VMEM_LIMIT),
        name="long_conv",
    )(x, fwd, inv, gate, kf)


def _regroup_to_tokens(blk_ref, scr, dil, width):
    m = blk_ref.shape[1]
    for c in range(width // LANES):
        for r in range(dil):
            scr[c, pl.ds(r, m, stride=dil), :] = blk_ref[r, :, c * LANES:(c + 1) * LANES]


def _merge_kernel(o0_ref, l0_ref, o1_ref, l1_ref, o2_ref, l2_ref, hy_ref, za_ref, ga_ref, gh_ref,
                  wa_ref, wh_ref, out_ref, so1, so2, sl1, sl2, shy, attn_scr, hy_scr, *, n2):
    d1, d2 = ATTN_GROUPS[1][1], ATTN_GROUPS[2][1]
    _regroup_to_tokens(o1_ref, so1, d1, ATTN_WIDTH)
    _regroup_to_tokens(o2_ref, so2, d2, ATTN_WIDTH)
    _regroup_to_tokens(l1_ref, sl1, d1, LANES)
    _regroup_to_tokens(l2_ref, sl2, d2, LANES)
    _regroup_to_tokens(hy_ref, shy, n2, HYENA_WIDTH)
    l0, l1, l2 = l0_ref[...], sl1[0], sl2[0]
    mx = jnp.maximum(jnp.maximum(l0, l1), l2)
    e0, e1, e2 = jnp.exp(l0 - mx), jnp.exp(l1 - mx), jnp.exp(l2 - mx)
    inv = 1.0 / (e0 + e1 + e2)
    a0, a1, a2 = e0 * inv, e1 * inv, e2 * inv
    for h in range(N_SLOTS):
        cs = slice(h * HEAD_DIM, (h + 1) * HEAD_DIM)
        mix = a0[:, h:h + 1] * o0_ref[:, cs] + a1[:, h:h + 1] * so1[h] + a2[:, h:h + 1] * so2[h]
        z = za_ref[:, cs].astype(jnp.float32)
        attn_scr[:, cs] = (mix * (z * _sigmoid(z))).astype(jnp.bfloat16)
        hy_scr[:, cs] = shy[h].astype(jnp.bfloat16)
    br_a = jnp.dot(attn_scr[...], wa_ref[...], preferred_element_type=jnp.float32)
    br_h = jnp.dot(hy_scr[...], wh_ref[...], preferred_element_type=jnp.float32)
    merged = (_sigmoid(ga_ref[...].astype(jnp.float32)) * br_a
              + _sigmoid(gh_ref[...].astype(jnp.float32)) * br_h)
    out_ref[...] = merged.astype(out_ref.dtype)


def _merge(outs, lses, hyena, gates, w_ba, w_bh, n2):
    B, d0, L, W = outs[0].shape
    D = w_ba.shape[1]
    tm = MERGE_TM
    d1, d2 = ATTN_GROUPS[1][1], ATTN_GROUPS[2][1]

    def cls(d, width):
        return pl.BlockSpec((None, d, tm // d, width), lambda b, i: (b, 0, i, 0))

    in_specs = [pl.BlockSpec((None, None, tm, W), lambda b, i: (b, 0, i, 0)),
                pl.BlockSpec((None, None, tm, LANES), lambda b, i: (b, 0, i, 0)),
                cls(d1, W), cls(d1, LANES), cls(d2, W), cls(d2, LANES), cls(n2, HYENA_WIDTH),
                pl.BlockSpec((None, tm, W), lambda b, i: (b, i, 0)),
                pl.BlockSpec((None, tm, D), lambda b, i: (b, i, 1)),
                pl.BlockSpec((None, tm, D), lambda b, i: (b, i, 2)),
                pl.BlockSpec(w_ba.shape, lambda b, i: (0, 0), pipeline_mode=pl.Buffered(1)),
                pl.BlockSpec(w_bh.shape, lambda b, i: (0, 0), pipeline_mode=pl.Buffered(1))]
    wt = W // LANES
    return pl.pallas_call(
        functools.partial(_merge_kernel, n2=n2),
        grid=(B, L // tm),
        in_specs=in_specs,
        out_specs=pl.BlockSpec((None, tm, D), lambda b, i: (b, i, 0)),
        out_shape=jax.ShapeDtypeStruct((B, L, D), jnp.bfloat16),
        scratch_shapes=[pltpu.VMEM((wt, tm, LANES), jnp.float32), pltpu.VMEM((wt, tm, LANES), jnp.float32),
                        pltpu.VMEM((1, tm, LANES), jnp.float32), pltpu.VMEM((1, tm, LANES), jnp.float32),
                        pltpu.VMEM((wt, tm, LANES), jnp.float32),
                        pltpu.VMEM((tm, W), jnp.bfloat16), pltpu.VMEM((tm, HYENA_WIDTH), jnp.bfloat16)],
        compiler_params=_cparams(2),
        name="merge",
    )(outs[0], lses[0], outs[1], lses[1], outs[2], lses[2], hyena, gates, gates, gates, w_ba, w_bh)


def _out_kernel(m_ref, w_ref, x_ref, g_ref, o_ref, *, final_norm):
    y = x_ref[...] + jnp.dot(m_ref[...], w_ref[...], preferred_element_type=jnp.float32)
    if final_norm:
        ms = jnp.mean(y * y, axis=-1, keepdims=True)
        y = y * lax.rsqrt(ms + RMS_EPS) * g_ref[...]
    o_ref[...] = y


def _out_proj(merged, w_out, x, final_gain, final_norm):
    B, L, D = x.shape
    tm = MERGE_TM
    tok = pl.BlockSpec((None, tm, D), lambda b, i: (b, i, 0))
    return pl.pallas_call(
        functools.partial(_out_kernel, final_norm=final_norm),
        grid=(B, L // tm),
        in_specs=[tok, pl.BlockSpec(w_out.shape, lambda b, i: (0, 0), pipeline_mode=pl.Buffered(1)),
                  tok, pl.BlockSpec((1, D), lambda b, i: (0, 0))],
        out_specs=tok,
        out_shape=jax.ShapeDtypeStruct((B, L, D), jnp.float32),
        compiler_params=_cparams(2),
        name="out_proj",
    )(merged, w_out, x, final_gain)


def _prep_layer(i, norm_gain, w_in, conv_w, conv_b, filt_w1, filt_b1, filt_w2, filt_b2, filt_w3, filt_b3,
                filt_w4, filt_freq, hyena_skip, w_branch_attn, w_branch_hyena, w_out):
    bf = jnp.bfloat16
    return dict(
        gain=norm_gain[i][None, :],
        w_in=w_in[i].astype(bf),
        conv_w=conv_w[i], conv_b=conv_b[i][None, :],
        filt_w1=filt_w1[i], filt_b1=filt_b1[i], filt_w2=filt_w2[i], filt_b2=filt_b2[i],
        filt_w3=filt_w3[i], filt_b3=filt_b3[i], filt_w4=filt_w4[i], filt_freq=filt_freq[i],
        hyena_skip=hyena_skip[i],
        w_ba=w_branch_attn[i].astype(bf), w_bh=w_branch_hyena[i].astype(bf), w_out=w_out[i].astype(bf))


def _layer(x, lp, final_gain, final_norm):
    B, L, D = x.shape
    assert L % (2 * DFT_HALF) == 0 and L % PROJ_TM == 0 and D == 2 * W_BLOCK
    n2 = L // DFT_HALF

    tabs = _rope_tables(L)
    outs, lses = [], []
    for g, (_, dil) in enumerate(ATTN_GROUPS):
        qkv = _proj_qkv(x, lp["gain"], lp["w_in"], tabs, dil, g)
        n = L // dil
        o, lse = _attention(qkv.reshape(3, B * dil, n, ATTN_WIDTH))
        outs.append(o.reshape(B, dil, n, ATTN_WIDTH))
        lses.append(lse.reshape(B, dil, n, LANES))

    gate_cols = (lambda s: jnp.where(s == 0, GATE_BLOCKS[0], GATE_BLOCKS[2] + 2 * (s - 1)),
                 lambda s: jnp.where(s == 0, GATE_BLOCKS[1], GATE_BLOCKS[2] + 2 * (s - 1) + 1))
    gates = _proj_plain(x, lp["gain"], lp["w_in"], gate_cols, 3, jnp.bfloat16)
    uhy_cols = tuple(functools.partial(lambda s, c: c, c=UHY_BLOCK + k) for k in range(3))
    u_hy = _proj_plain(x, lp["gain"], lp["w_in"], uhy_cols, 1, jnp.bfloat16)
    gated, mult = _hyena_pre(u_hy, gates, lp["conv_w"], lp["conv_b"], n2)
    fwd_tabs, inv_tabs = _dft_tables(n2)
    kf = _filter_spectrum(_filter(L, n2, lp), fwd_tabs)
    hyena = _long_conv(gated, mult, fwd_tabs, inv_tabs, kf)

    merged = _merge(outs, lses, hyena, gates, lp["w_ba"], lp["w_bh"], n2)
    return _out_proj(merged, lp["w_out"], x, final_gain, final_norm)


def kernel(x_prompt, x_sample, norm_gain, w_in, conv_w, conv_b, filt_w1, filt_b1, filt_w2, filt_b2, filt_w3, filt_b3, filt_w4, filt_freq, hyena_skip, w_branch_attn, w_branch_hyena, w_out, final_gain):
    depth = w_in.shape[0]
    layers = [_prep_layer(i, norm_gain, w_in, conv_w, conv_b, filt_w1, filt_b1, filt_w2, filt_b2, filt_w3,
                          filt_b3, filt_w4, filt_freq, hyena_skip, w_branch_attn, w_branch_hyena, w_out)
              for i in range(depth)]
    fg = final_gain[None, :]

    def trunk(x):
        for i, lp in enumerate(layers):
            x = _layer(x, lp, fg, final_norm=(i == depth - 1))
        return x

    return (trunk(x_prompt), trunk(x_sample))
```

```python
import functools
import math

import jax
import jax.numpy as jnp
import numpy as np
from jax import lax
from jax.experimental import pallas as pl
from jax.experimental.pallas import tpu as pltpu

HEAD_DIM = 128
N_SLOTS = 8
ATTN_GROUPS = ((128, 1), (512, 4), (2048, 16))
N_GROUPS = len(ATTN_GROUPS)
ATTN_WIDTH = N_SLOTS * HEAD_DIM
QKV_WIDTH = N_GROUPS * ATTN_WIDTH
ROT_DIM = HEAD_DIM // 4
ROPE_THETA = 500000.0
HYENA_WIDTH = 1024
FILTER_EMB = 33
FILTER_HIDDEN = 64
DECAY_TARGET = 1e-2
DECAY_FAST_PCT = 0.3
DECAY_SLOW_PCT = 1.5
RMS_EPS = 1e-6
RADIUS = 64

LANES = 128
SUBLANES = 8
VMEM_LIMIT = 56 * 1024 * 1024
LONG_CONV_VMEM = 36 * 1024 * 1024
LONG_CONV_RESIDENT_VMEM = 48 * 1024 * 1024

DFT_N1 = 512
DFT_HALF = DFT_N1 // 2
DFT_ROWS = 264
DFT_M = 2 * DFT_ROWS

PROJ_TM = 512
W_BLOCK = 1024
_ZA_BLOCK = 3 * QKV_WIDTH // W_BLOCK
UHY_BLOCK = _ZA_BLOCK + ATTN_WIDTH // W_BLOCK
_ZH_BLOCK = UHY_BLOCK + 3 * HYENA_WIDTH // W_BLOCK
GATE_BLOCKS = (_ZA_BLOCK, _ZH_BLOCK, _ZH_BLOCK + HYENA_WIDTH // W_BLOCK)
PERM_MIN_DIL = 8
ATTN_TQ = 512
ATTN_SUB = 128
HY_TL = 512
HALO_ROWS = 16
MERGE_TM = 512


def _cparams(n_axes):
    return pltpu.CompilerParams(dimension_semantics=("parallel",) * n_axes,
                                vmem_limit_bytes=VMEM_LIMIT)


def _sigmoid(x):
    return 0.5 * jnp.tanh(0.5 * x) + 0.5


def _split_bf16(x):
    hi = x.astype(jnp.bfloat16)
    lo = (x - hi.astype(jnp.float32)).astype(jnp.bfloat16)
    return hi, lo


def _normed(x_ref, g_ref, rows=slice(None)):
    x = x_ref[rows, :]
    ms = jnp.mean(x * x, axis=-1, keepdims=True)
    return (x * lax.rsqrt(ms + RMS_EPS) * g_ref[...]).astype(jnp.bfloat16)


def _proj_plain_kernel(x_ref, g_ref, *rest):
    w_refs, o_ref = rest[:-1], rest[-1]
    tm = x_ref.shape[0]
    for half in range(2):
        rows = slice(half * tm // 2, (half + 1) * tm // 2)
        h = _normed(x_ref, g_ref, rows)
        for k, w_ref in enumerate(w_refs):
            cw = w_ref.shape[1]
            o_ref[rows, k * cw:(k + 1) * cw] = jnp.dot(
                h, w_ref[...], preferred_element_type=jnp.float32).astype(o_ref.dtype)


def _proj_plain(x, gain, wb, col_maps, n_slabs, out_dtype):
    B, L, D = x.shape
    tm = 2 * PROJ_TM
    cs = len(col_maps) * W_BLOCK
    w_specs = [pl.BlockSpec((D, W_BLOCK), functools.partial(lambda s, b, i, f: (0, f(s)), f=f),
                            pipeline_mode=pl.Buffered(1)) for f in col_maps]
    return pl.pallas_call(
        _proj_plain_kernel,
        grid=(n_slabs, B, L // tm),
        in_specs=[pl.BlockSpec((None, tm, D), lambda s, b, i: (b, i, 0)),
                  pl.BlockSpec((1, D), lambda s, b, i: (0, 0))] + w_specs,
        out_specs=pl.BlockSpec((None, tm, cs), lambda s, b, i: (b, i, s)),
        out_shape=jax.ShapeDtypeStruct((B, L, n_slabs * cs), out_dtype),
        compiler_params=_cparams(3),
        name="proj_plain",
    )(x, gain, *([wb] * len(col_maps)))


def _proj_qkv_kernel(x_ref, g_ref, wq_ref, wk_ref, wv_ref, cq_ref, sq_ref, ck_ref, sk_ref, *rest, dil, perm):
    tm = PROJ_TM
    rows = tm // dil
    if perm:
        p_ref, o_ref = rest
    elif dil > 1:
        o_ref, scr = rest
    else:
        (o_ref,) = rest
    lane = lax.broadcasted_iota(jnp.int32, (tm, HEAD_DIM), 1)
    first_half = lane < (ROT_DIM // 2)
    for half in range(x_ref.shape[0] // tm):
        tok = slice(half * tm, (half + 1) * tm)
        h = _normed(x_ref, g_ref, tok)
        if perm:
            h = jnp.dot(p_ref[...], h, preferred_element_type=jnp.float32).astype(jnp.bfloat16)
        for which, w_ref in enumerate((wq_ref, wk_ref, wv_ref)):
            acc = jnp.dot(h, w_ref[...], preferred_element_type=jnp.float32)
            for slot in range(N_SLOTS):
                cs = slice(slot * HEAD_DIM, (slot + 1) * HEAD_DIM)
                t = acc[:, cs]
                if which < 2:
                    cos_ref, sin_ref = (cq_ref, sq_ref) if which == 0 else (ck_ref, sk_ref)
                    partner = jnp.where(first_half,
                                        pltpu.roll(t, HEAD_DIM - ROT_DIM // 2, axis=1),
                                        pltpu.roll(t, ROT_DIM // 2, axis=1))
                    t = t * cos_ref[tok, :] + partner * sin_ref[tok, :]
                if perm or dil == 1:
                    for r in range(dil):
                        o_ref[which, r, half * rows:(half + 1) * rows, cs] = (
                            t[r * rows:(r + 1) * rows].astype(o_ref.dtype))
                else:
                    scr[slot] = t
            if not perm and dil > 1:
                for slot in range(N_SLOTS):
                    for r in range(dil):
                        o_ref[which, r, half * rows:(half + 1) * rows, slot * HEAD_DIM:(slot + 1) * HEAD_DIM] = (
                            scr[slot, pl.ds(r, rows, stride=dil), :].astype(o_ref.dtype))


def _class_major(a, tm, dil):
    n_tiles = a.shape[0] // tm
    return a.reshape(n_tiles, tm // dil, dil, -1).transpose(0, 2, 1, 3).reshape(a.shape)


def _class_major_onehot(tm, dil):
    src = np.arange(tm).reshape(tm // dil, dil).T.reshape(tm)
    onehot = np.zeros((tm, tm), np.float32)
    onehot[np.arange(tm), src] = 1.0
    return jnp.asarray(onehot, jnp.bfloat16)


def _proj_qkv(x, gain, wb, tabs, dil, group):
    B, L, D = x.shape
    half = PROJ_TM
    tm = 2 * half
    n = L // dil
    perm = dil >= PERM_MIN_DIL
    tab_spec = pl.BlockSpec((tm, HEAD_DIM), lambda b, i: (i, 0))
    w_specs = [pl.BlockSpec((D, W_BLOCK), functools.partial(lambda b, i, c: (0, c), c=t * N_GROUPS + group),
                            pipeline_mode=pl.Buffered(1)) for t in range(3)]
    in_specs = [pl.BlockSpec((None, tm, D), lambda b, i: (b, i, 0)),
                pl.BlockSpec((1, D), lambda b, i: (0, 0))] + w_specs + [tab_spec] * 4
    args = [x, gain, wb, wb, wb]
    scratch = []
    if perm:
        args += [_class_major(t, half, dil) for t in tabs] + [_class_major_onehot(half, dil)]
        in_specs.append(pl.BlockSpec((half, half), lambda b, i: (0, 0), pipeline_mode=pl.Buffered(1)))
    else:
        args += list(tabs)
        if dil > 1:
            scratch = [pltpu.VMEM((N_SLOTS, half, HEAD_DIM), jnp.float32)]
    return pl.pallas_call(
        functools.partial(_proj_qkv_kernel, dil=dil, perm=perm),
        grid=(B, L // tm),
        in_specs=in_specs,
        out_specs=pl.BlockSpec((3, None, dil, tm // dil, ATTN_WIDTH), lambda b, i: (0, b, 0, i, 0)),
        out_shape=jax.ShapeDtypeStruct((3, B, dil, n, ATTN_WIDTH), jnp.bfloat16),
        scratch_shapes=scratch,
        compiler_params=_cparams(2),
        name=f"proj_qkv_d{dil}",
    )(*args)


def _rope_tables(L):
    inv_freq = jnp.power(ROPE_THETA, -jnp.arange(0, ROT_DIM, 2, dtype=jnp.float32) / ROT_DIM)
    ang = jnp.arange(L, dtype=jnp.float32)[:, None] * inv_freq[None, :]
    cos, sin = jnp.cos(ang), jnp.sin(ang)
    ones = jnp.ones((L, HEAD_DIM - ROT_DIM), jnp.float32)
    cos_t = jnp.concatenate([cos, cos, ones], axis=1)
    sin_t = jnp.concatenate([-sin, sin, 0.0 * ones], axis=1)
    scale = 1.0 / math.sqrt(HEAD_DIM)
    return cos_t * scale, sin_t * scale, cos_t, sin_t


def _attn_kernel(q_ref, kp_ref, kc_ref, kn_ref, vp_ref, vc_ref, vn_ref, o_ref, lse_ref, kw, vw, *, n):
    tq = q_ref.shape[0]
    i = pl.program_id(1)
    kw[0:RADIUS] = kp_ref[...]
    kw[RADIUS:RADIUS + tq] = kc_ref[...]
    kw[RADIUS + tq:] = kn_ref[...]
    vw[0:RADIUS] = vp_ref[...]
    vw[RADIUS:RADIUS + tq] = vc_ref[...]
    vw[RADIUS + tq:] = vn_ref[...]
    win = ATTN_SUB + 2 * RADIUS
    row = lax.broadcasted_iota(jnp.int32, (ATTN_SUB, win), 0)
    col = lax.broadcasted_iota(jnp.int32, (ATTN_SUB, win), 1)
    band = (col >= row) & (col <= row + 2 * RADIUS)
    lane = lax.broadcasted_iota(jnp.int32, (ATTN_SUB, LANES), 1)
    nsub = tq // ATTN_SUB
    for s in range(nsub):
        q0 = i * tq + s * ATTN_SUB
        valid = band
        if s == 0:
            valid = valid & (col >= RADIUS - q0)
        if s == nsub - 1:
            valid = valid & (col < n - q0 + RADIUS)
        lse_tile = jnp.zeros((ATTN_SUB, LANES), jnp.float32)
        for h in range(N_SLOTS):
            cs = slice(h * HEAD_DIM, (h + 1) * HEAD_DIM)
            qb = q_ref[s * ATTN_SUB:(s + 1) * ATTN_SUB, cs]
            kb = kw[s * ATTN_SUB:s * ATTN_SUB + win, cs]
            vb = vw[s * ATTN_SUB:s * ATTN_SUB + win, cs]
            sc = lax.dot_general(qb, kb, (((1,), (1,)), ((), ())), preferred_element_type=jnp.float32)
            sc = jnp.where(valid, sc, -1e30)
            m = jnp.max(sc, axis=1, keepdims=True)
            p = jnp.exp(sc - m)
            l = jnp.sum(p, axis=1, keepdims=True)
            o = jnp.dot(p.astype(jnp.bfloat16), vb, preferred_element_type=jnp.float32)
            o_ref[s * ATTN_SUB:(s + 1) * ATTN_SUB, cs] = o / l
            lse_tile = jnp.where(lane == h, m + jnp.log(l), lse_tile)
        lse_ref[s * ATTN_SUB:(s + 1) * ATTN_SUB, :] = lse_tile


def _attention(qkv):
    _, S, n, W = qkv.shape
    tq = min(ATTN_TQ, n)
    r = tq // RADIUS
    last = n // RADIUS - 1

    def cur(which):
        return pl.BlockSpec((None, None, tq, W), lambda s, i: (which, s, i, 0))

    def prev(which):
        return pl.BlockSpec((None, None, RADIUS, W), lambda s, i: (which, s, jnp.maximum(i * r - 1, 0), 0))

    def nxt(which):
        return pl.BlockSpec((None, None, RADIUS, W), lambda s, i: (which, s, jnp.minimum((i + 1) * r, last), 0))

    return pl.pallas_call(
        functools.partial(_attn_kernel, n=n),
        grid=(S, n // tq),
        in_specs=[cur(0), prev(1), cur(1), nxt(1), prev(2), cur(2), nxt(2)],
        out_specs=[pl.BlockSpec((None, tq, W), lambda s, i: (s, i, 0)),
                   pl.BlockSpec((None, tq, LANES), lambda s, i: (s, i, 0))],
        out_shape=[jax.ShapeDtypeStruct((S, n, W), jnp.float32),
                   jax.ShapeDtypeStruct((S, n, LANES), jnp.float32)],
        scratch_shapes=[pltpu.VMEM((tq + 2 * RADIUS, W), jnp.bfloat16),
                        pltpu.VMEM((tq + 2 * RADIUS, W), jnp.bfloat16)],
        compiler_params=_cparams(2),
        name="attn",
    )(qkv, qkv, qkv, qkv, qkv, qkv, qkv)


def _shift_rows(u, edge_row, down):
    tl = u.shape[0]
    rolled = pltpu.roll(u, 1 if down else tl - 1, axis=0)
    r8 = lax.broadcasted_iota(jnp.int32, (SUBLANES, u.shape[1]), 0)
    if down:
        head = jnp.where(r8 == 0, edge_row, rolled[:SUBLANES])
        return jnp.concatenate([head, rolled[SUBLANES:]], axis=0)
    tail = jnp.where(r8 == SUBLANES - 1, edge_row, rolled[tl - SUBLANES:])
    return jnp.concatenate([rolled[:tl - SUBLANES], tail], axis=0)


def _hyena_pre_kernel(u_ref, up_ref, un_ref, z_ref, cw_ref, cb_ref, p_ref, g_ref, m_ref, *, n2, nblk):
    tl = u_ref.shape[0]
    i = pl.program_id(1)
    has_prev = (i > 0).astype(jnp.float32)
    has_next = (i < nblk - 1).astype(jnp.float32)
    C = HYENA_WIDTH
    rows = tl // n2
    perm = p_ref[...]

    def conv(cs):
        u = u_ref[:, cs].astype(jnp.float32)
        edge_prev = up_ref[HALO_ROWS - 1:HALO_ROWS, cs].astype(jnp.float32) * has_prev
        edge_next = un_ref[0:1, cs].astype(jnp.float32) * has_next
        u_prev = _shift_rows(u, edge_prev, True)
        u_next = _shift_rows(u, edge_next, False)
        return u_prev * cw_ref[0:1, cs] + u * cw_ref[1:2, cs] + u_next * cw_ref[2:3, cs] + cb_ref[:, cs]

    for c in range(C // LANES):
        ls = slice(c * LANES, (c + 1) * LANES)
        x0, x1, vh = (conv(slice(k * C + c * LANES, k * C + (c + 1) * LANES)) for k in range(3))
        z = z_ref[:, ls].astype(jnp.float32)
        m_hi, m_lo = _split_bf16(x0 * (z * _sigmoid(z)))
        stack = jnp.concatenate([(vh * x1).astype(jnp.bfloat16), m_hi, m_lo], axis=1)
        res = jnp.dot(perm, stack, preferred_element_type=jnp.float32)
        g = res[:, :LANES]
        m = res[:, LANES:2 * LANES] + res[:, 2 * LANES:]
        for r in range(n2):
            g_ref[r, :, ls] = g[r * rows:(r + 1) * rows]
            m_ref[r, :, ls] = m[r * rows:(r + 1) * rows]


def _hyena_pre(u_hy, gates, conv_w, conv_b, n2):
    B, L, C3 = u_hy.shape
    C = HYENA_WIDTH
    tl = HY_TL
    nblk = L // tl
    per = tl // HALO_ROWS
    last_halo = L // HALO_ROWS - 1
    out_spec = pl.BlockSpec((None, n2, tl // n2, C), lambda b, i: (b, 0, i, 0))
    out_sds = jax.ShapeDtypeStruct((B, n2, L // n2, C), jnp.float32)
    return pl.pallas_call(
        functools.partial(_hyena_pre_kernel, n2=n2, nblk=nblk),
        grid=(B, nblk),
        in_specs=[pl.BlockSpec((None, tl, C3), lambda b, i: (b, i, 0)),
                  pl.BlockSpec((None, HALO_ROWS, C3), lambda b, i: (b, jnp.maximum(i * per - 1, 0), 0)),
                  pl.BlockSpec((None, HALO_ROWS, C3), lambda b, i: (b, jnp.minimum((i + 1) * per, last_halo), 0)),
                  pl.BlockSpec((None, tl, C), lambda b, i: (b, i, 1)),
                  pl.BlockSpec((3, C3), lambda b, i: (0, 0)),
                  pl.BlockSpec((1, C3), lambda b, i: (0, 0)),
                  pl.BlockSpec((tl, tl), lambda b, i: (0, 0), pipeline_mode=pl.Buffered(1))],
        out_specs=[out_spec, out_spec],
        out_shape=[out_sds, out_sds],
        compiler_params=_cparams(2),
        name="hyena_pre",
    )(u_hy, u_hy, u_hy, gates, conv_w, conv_b, _class_major_onehot(tl, n2))


POS_VALID, POS_ZERO = FILTER_EMB, FILTER_EMB + 1


def _dot3(a, wh_ref, wl_ref):
    ah, al = _split_bf16(a)
    wh = wh_ref[...]
    return (jnp.dot(ah, wh, preferred_element_type=jnp.float32)
            + jnp.dot(al, wh, preferred_element_type=jnp.float32)
            + jnp.dot(ah, wl_ref[...], preferred_element_type=jnp.float32))


def _filter_kernel(p_ref, w1h, w1l, b1_ref, w2h, w2l, b2_ref, w3h, w3l, b3_ref, w4fh, w4fl, w4bh, w4bl,
                   fr_ref, dl_ref, sk_ref, o_ref):
    p = p_ref[...]
    fr = fr_ref[...]
    p2 = jnp.concatenate([p[:DFT_HALF], p[DFT_HALF:]], axis=1)
    hdn = jnp.sin(fr * (_dot3(p2, w1h, w1l) + b1_ref[...]))
    hdn = jnp.sin(fr * (_dot3(hdn, w2h, w2l) + b2_ref[...]))
    hdn = jnp.sin(fr * (_dot3(hdn, w3h, w3l) + b3_ref[...]))
    scale = jnp.exp(-p[:, 0:1] * dl_ref[...]) * p[:, POS_VALID:POS_VALID + 1]
    skip = p[:, POS_ZERO:POS_ZERO + 1] * sk_ref[...]
    for d, (wh, wl) in enumerate(((w4fh, w4fl), (w4bh, w4bl))):
        rs = slice(d * DFT_HALF, (d + 1) * DFT_HALF)
        o_ref[rs, :] = (_dot3(hdn, wh, wl) * scale[rs] + skip[rs]).astype(o_ref.dtype)


def _filter_lag_table(L, n2):
    bands = (FILTER_EMB - 1) // 2
    f32 = np.float32
    n = np.arange(2 * L).reshape(2 * L // n2, n2).T.reshape(2 * L)
    lag = np.where(n < L, n, 2 * L - n)
    idx = np.where(n == L, 0, lag).astype(f32)[:, None]
    t = np.where(idx == L - 1, f32(1.0), idx * f32(1.0 / (L - 1))).astype(f32)
    w = (f32(2.0 * math.pi) * idx / f32(L)).astype(f32)
    f = np.linspace(1e-4, bands - 1, bands, dtype=f32)[None, :]
    arg = (f * w).astype(f32)
    flags = np.stack([n != L, n == 0], axis=1).astype(f32)
    tab = np.concatenate([t, np.cos(arg), -np.sin(arg), flags], axis=1).astype(f32)
    return jnp.pad(jnp.asarray(tab), ((0, 0), (0, LANES - tab.shape[1])))


def _filter(L, n2, lp):
    C = HYENA_WIDTH
    tab = _filter_lag_table(L, n2)
    H = FILTER_HIDDEN
    assert 2 * H == LANES
    zeros = lambda r, c: jnp.zeros((r, c), jnp.float32)
    blockdiag = lambda w: jnp.block([[w, zeros(*w.shape)], [zeros(*w.shape), w]])
    w1p = jnp.concatenate([lp["filt_w1"], zeros(LANES - FILTER_EMB, H)], axis=0)
    w4 = lp["filt_w4"]
    w4f = jnp.concatenate([w4[:, :C], zeros(H, C)], axis=0)
    w4b = jnp.concatenate([zeros(H, C), w4[:, C:]], axis=0)
    twice = lambda v: jnp.concatenate([v, v])[None, :]
    min_decay = math.log(DECAY_TARGET) / DECAY_FAST_PCT
    max_decay = math.log(DECAY_TARGET) / DECAY_SLOW_PCT
    deltas = jnp.abs(jnp.linspace(min_decay, max_decay, C, dtype=jnp.float32))[None, :]
    full = lambda a: pl.BlockSpec(a.shape, lambda i: (0,) * a.ndim)
    args = [*_split_bf16(blockdiag(w1p)), twice(lp["filt_b1"]), *_split_bf16(blockdiag(lp["filt_w2"])),
            twice(lp["filt_b2"]), *_split_bf16(blockdiag(lp["filt_w3"])), twice(lp["filt_b3"]),
            *_split_bf16(w4f), *_split_bf16(w4b), twice(lp["filt_freq"]), deltas, lp["hyena_skip"][None, :]]
    k = pl.pallas_call(
        _filter_kernel,
        grid=(n2,),
        in_specs=[pl.BlockSpec((DFT_N1, LANES), lambda i: (i, 0))] + [full(a) for a in args],
        out_specs=pl.BlockSpec((DFT_N1, C), lambda i: (i, 0)),
        out_shape=jax.ShapeDtypeStruct((2 * L, C), jnp.bfloat16),
        compiler_params=_cparams(1),
        name="filter",
    )(tab, *args)
    return k.reshape(n2, DFT_N1, C)


def _dft_tables(n2):
    N = DFT_N1 * n2
    k1 = np.arange(DFT_ROWS)
    live = (k1 <= DFT_HALF).astype(np.float64)
    a = 2.0 * np.pi * ((k1[:, None] * np.arange(DFT_N1)[None, :]) % DFT_N1) / DFT_N1
    b = 2.0 * np.pi * (k1[None, :] * np.arange(n2)[:, None]) / N
    ca, sa = (jnp.asarray(t * live[:, None], jnp.float32) for t in (np.cos(a), np.sin(a)))
    cb, sb = (jnp.asarray(t, jnp.float32) for t in (np.cos(b), np.sin(b)))
    f_re = ca[None] * cb[:, :, None] - sa[None] * sb[:, :, None]
    f_im = -(sa[None] * cb[:, :, None] + ca[None] * sb[:, :, None])
    F = jnp.concatenate([f_re, f_im], axis=1).astype(jnp.bfloat16)
    weight = np.where((k1 == 0) | (k1 == DFT_HALF), 1.0, 2.0) / N
    cat, sat = (jnp.asarray((t * live[:, None] * weight[:, None]).T[:DFT_HALF], jnp.float32)
                for t in (np.cos(a), np.sin(a)))
    g_re = cat[None] * cb[:, None, :] - sat[None] * sb[:, None, :]
    g_im = -(sat[None] * cb[:, None, :] + cat[None] * sb[:, None, :])
    G = jnp.concatenate([g_re, g_im], axis=2).astype(jnp.bfloat16)
    return F, G


def _filter_spectrum_kernel(k_ref, f_ref, o_ref, *, n2, slabs):
    nj = n2 // slabs
    j = pl.program_id(1)
    for s in range(slabs):
        o_ref[j * slabs + s] = jnp.dot(f_ref[s], k_ref[s], preferred_element_type=jnp.float32)

    @pl.when(j == nj - 1)
    def _across_slabs():
        def body(c, carry):
            re = pl.ds(pl.multiple_of(c * SUBLANES, SUBLANES), SUBLANES)
            im = pl.ds(pl.multiple_of(c * SUBLANES + DFT_ROWS, SUBLANES), SUBLANES)
            for t in range(o_ref.shape[2] // LANES):
                ls = slice(t * LANES, (t + 1) * LANES)
                ys = _fft_list([(o_ref[r, re, ls], o_ref[r, im, ls]) for r in range(n2)], -1)
                for k2 in range(n2):
                    o_ref[k2, re, ls] = ys[k2][0]
                    o_ref[k2, im, ls] = ys[k2][1]
            return carry

        lax.fori_loop(0, DFT_ROWS // SUBLANES, body, 0)


def _filter_spectrum(k, fwd, slabs=8):
    n2, K, C = k.shape
    slabs = min(slabs, n2)
    tc = 2 * LANES
    return pl.pallas_call(
        functools.partial(_filter_spectrum_kernel, n2=n2, slabs=slabs),
        grid=(C // tc, n2 // slabs),
        in_specs=[pl.BlockSpec((slabs, K, tc), lambda c, j: (j, 0, c)),
                  pl.BlockSpec((slabs, DFT_M, K), lambda c, j: (j, 0, 0))],
        out_specs=pl.BlockSpec((n2, DFT_M, tc), lambda c, j: (0, 0, c), pipeline_mode=pl.Buffered(1)),
        out_shape=jax.ShapeDtypeStruct((n2, DFT_M, C), jnp.float32),
        compiler_params=pltpu.CompilerParams(dimension_semantics=("parallel", "arbitrary"),
                                             vmem_limit_bytes=VMEM_LIMIT),
        name="filter_spectrum",
    )(k, fwd)


def _fft_list(xs, sign):
    n = len(xs)
    if n == 1:
        return xs
    even = _fft_list(xs[0::2], sign)
    odd = _fft_list(xs[1::2], sign)
    out = [None] * n
    for k in range(n // 2):
        er, ei = even[k]
        pr, pi = odd[k]
        if k == 0:
            tr, ti = pr, pi
        elif 4 * k == n:
            tr, ti = (pi, -pr) if sign < 0 else (-pi, pr)
        else:
            ang = sign * 2.0 * math.pi * k / n
            wr, wi = math.cos(ang), math.sin(ang)
            tr, ti = pr * wr - pi * wi, pr * wi + pi * wr
        out[k] = (er + tr, ei + ti)
        out[k + n // 2] = (er - tr, ei - ti)
    return out


def _long_conv_kernel(x_ref, f_ref, g_ref, gate_ref, kf_ref, o_ref, spec, *, n2, slabs, resident):
    nj = n2 // slabs
    j = pl.program_id(2)
    tc = spec.shape[2]
    fwd_base = j * slabs if resident else 0
    inv_base = (j - nj) * slabs if resident else 0

    @pl.when(j < nj)
    def _forward():
        for s in range(slabs):
            spec[j * slabs + s] = jnp.dot(f_ref[fwd_base + s], x_ref[s].astype(jnp.bfloat16),
                                          preferred_element_type=jnp.float32)

    @pl.when(j == nj - 1)
    def _across_slabs():
        def body(c, carry):
            re = pl.ds(pl.multiple_of(c * SUBLANES, SUBLANES), SUBLANES)
            im = pl.ds(pl.multiple_of(c * SUBLANES + DFT_ROWS, SUBLANES), SUBLANES)
            for t in range(tc // LANES):
                ls = slice(t * LANES, (t + 1) * LANES)
                ys = _fft_list([(spec[r, re, ls], spec[r, im, ls]) for r in range(n2)], -1)
                prod = []
                for k2 in range(n2):
                    yr, yi = ys[k2]
                    fr, fi = kf_ref[k2, re, ls], kf_ref[k2, im, ls]
                    prod.append((yr * fr - yi * fi, yr * fi + yi * fr))
                vs = _fft_list(prod, +1)
                for r in range(n2):
                    spec[r, re, ls] = vs[r][0]
                    spec[r, im, ls] = vs[r][1]
            return carry

        lax.fori_loop(0, DFT_ROWS // SUBLANES, body, 0)

    @pl.when(j >= nj)
    def _inverse():
        for s in range(slabs):
            v = spec[(j - nj) * slabs + s].astype(jnp.bfloat16)
            o_ref[inv_base + s] = (jnp.dot(g_ref[inv_base + s], v, preferred_element_type=jnp.float32)
                                   * gate_ref[inv_base + s])


def _long_conv(x, gate, fwd, inv, kf, slabs=8):
    B, n2, K, C = x.shape
    slabs = min(slabs, n2)
    nj = n2 // slabs
    spectra = lambda tc: 2 * n2 * DFT_M * tc * 4
    tc = 2 * LANES if spectra(2 * LANES) <= LONG_CONV_VMEM else LANES
    whole = 2 * n2 * DFT_M * K * 2 + 2 * 2 * n2 * K * tc * 4
    resident = spectra(tc) + whole + 2 * slabs * K * tc * 4 <= LONG_CONV_RESIDENT_VMEM
    fwd_j = lambda j: jnp.minimum(j, nj - 1)
    inv_j = lambda j: jnp.maximum(j - nj, 0)
    if resident:
        f_spec = pl.BlockSpec((n2, DFT_M, K), lambda c, b, j: (0, 0, 0), pipeline_mode=pl.Buffered(1))
        g_spec = pl.BlockSpec((n2, K, DFT_M), lambda c, b, j: (0, 0, 0), pipeline_mode=pl.Buffered(1))
        io_spec = pl.BlockSpec((None, n2, K, tc), lambda c, b, j: (b, 0, 0, c))
    else:
        f_spec = pl.BlockSpec((slabs, DFT_M, K), lambda c, b, j: (fwd_j(j), 0, 0))
        g_spec = pl.BlockSpec((slabs, K, DFT_M), lambda c, b, j: (inv_j(j), 0, 0))
        io_spec = pl.BlockSpec((None, slabs, K, tc), lambda c, b, j: (b, inv_j(j), 0, c))
    return pl.pallas_call(
        functools.partial(_long_conv_kernel, n2=n2, slabs=slabs, resident=resident),
        grid=(C // tc, B, 2 * nj),
        in_specs=[pl.BlockSpec((None, slabs, K, tc), lambda c, b, j: (b, fwd_j(j), 0, c)),
                  f_spec, g_spec, io_spec,
                  pl.BlockSpec((n2, DFT_M, tc), lambda c, b, j: (0, 0, c), pipeline_mode=pl.Buffered(1))],
        out_specs=io_spec,
        out_shape=jax.ShapeDtypeStruct((B, n2, K, C), jnp.float32),
        scratch_shapes=[pltpu.VMEM((n2, DFT_M, tc), jnp.float32)],
        compiler_params=pltpu.CompilerParams(dimension_semantics=("parallel", "parallel", "arbitrary"),
                                             vmem_limit_bytes=VMEM_LIMIT),
        name="long_conv",
    )(x, fwd, inv, gate, kf)


def _regroup_to_tokens(blk_ref, scr, dil, width):
    m = blk_ref.shape[1]
    for c in range(width // LANES):
        for r in range(dil):
            scr[c, pl.ds(r, m, stride=dil), :] = blk_ref[r, :, c * LANES:(c + 1) * LANES]


def _merge_kernel(o0_ref, l0_ref, o1_ref, l1_ref, o2_ref, l2_ref, hy_ref, za_ref, ga_ref, gh_ref,
                  wa_ref, wh_ref, out_ref, so1, so2, sl1, sl2, shy, attn_scr, hy_scr, *, n2):
    d1, d2 = ATTN_GROUPS[1][1], ATTN_GROUPS[2][1]
    _regroup_to_tokens(o1_ref, so1, d1, ATTN_WIDTH)
    _regroup_to_tokens(o2_ref, so2, d2, ATTN_WIDTH)
    _regroup_to_tokens(l1_ref, sl1, d1, LANES)
    _regroup_to_tokens(l2_ref, sl2, d2, LANES)
    _regroup_to_tokens(hy_ref, shy, n2, HYENA_WIDTH)
    l0, l1, l2 = l0_ref[...], sl1[0], sl2[0]
    mx = jnp.maximum(jnp.maximum(l0, l1), l2)
    e0, e1, e2 = jnp.exp(l0 - mx), jnp.exp(l1 - mx), jnp.exp(l2 - mx)
    inv = 1.0 / (e0 + e1 + e2)
    a0, a1, a2 = e0 * inv, e1 * inv, e2 * inv
    for h in range(N_SLOTS):
        cs = slice(h * HEAD_DIM, (h + 1) * HEAD_DIM)
        mix = a0[:, h:h + 1] * o0_ref[:, cs] + a1[:, h:h + 1] * so1[h] + a2[:, h:h + 1] * so2[h]
        z = za_ref[:, cs].astype(jnp.float32)
        attn_scr[:, cs] = (mix * (z * _sigmoid(z))).astype(jnp.bfloat16)
        hy_scr[:, cs] = shy[h].astype(jnp.bfloat16)
    br_a = jnp.dot(attn_scr[...], wa_ref[...], preferred_element_type=jnp.float32)
    br_h = jnp.dot(hy_scr[...], wh_ref[...], preferred_element_type=jnp.float32)
    merged = (_sigmoid(ga_ref[...].astype(jnp.float32)) * br_a
              + _sigmoid(gh_ref[...].astype(jnp.float32)) * br_h)
    out_ref[...] = merged.astype(out_ref.dtype)


def _merge(outs, lses, hyena, gates, w_ba, w_bh, n2):
    B, d0, L, W = outs[0].shape
    D = w_ba.shape[1]
    tm = MERGE_TM
    d1, d2 = ATTN_GROUPS[1][1], ATTN_GROUPS[2][1]

    def cls(d, width):
        return pl.BlockSpec((None, d, tm // d, width), lambda b, i: (b, 0, i, 0))

    in_specs = [pl.BlockSpec((None, None, tm, W), lambda b, i: (b, 0, i, 0)),
                pl.BlockSpec((None, None, tm, LANES), lambda b, i: (b, 0, i, 0)),
                cls(d1, W), cls(d1, LANES), cls(d2, W), cls(d2, LANES), cls(n2, HYENA_WIDTH),
                pl.BlockSpec((None, tm, W), lambda b, i: (b, i, 0)),
                pl.BlockSpec((None, tm, D), lambda b, i: (b, i, 1)),
                pl.BlockSpec((None, tm, D), lambda b, i: (b, i, 2)),
                pl.BlockSpec(w_ba.shape, lambda b, i: (0, 0), pipeline_mode=pl.Buffered(1)),
                pl.BlockSpec(w_bh.shape, lambda b, i: (0, 0), pipeline_mode=pl.Buffered(1))]
    wt = W // LANES
    return pl.pallas_call(
        functools.partial(_merge_kernel, n2=n2),
        grid=(B, L // tm),
        in_specs=in_specs,
        out_specs=pl.BlockSpec((None, tm, D), lambda b, i: (b, i, 0)),
        out_shape=jax.ShapeDtypeStruct((B, L, D), jnp.bfloat16),
        scratch_shapes=[pltpu.VMEM((wt, tm, LANES), jnp.float32), pltpu.VMEM((wt, tm, LANES), jnp.float32),
                        pltpu.VMEM((1, tm, LANES), jnp.float32), pltpu.VMEM((1, tm, LANES), jnp.float32),
                        pltpu.VMEM((wt, tm, LANES), jnp.float32),
                        pltpu.VMEM((tm, W), jnp.bfloat16), pltpu.VMEM((tm, HYENA_WIDTH), jnp.bfloat16)],
        compiler_params=_cparams(2),
        name="merge",
    )(outs[0], lses[0], outs[1], lses[1], outs[2], lses[2], hyena, gates, gates, gates, w_ba, w_bh)


def _out_kernel(m_ref, w_ref, x_ref, g_ref, o_ref, *, final_norm):
    y = x_ref[...] + jnp.dot(m_ref[...], w_ref[...], preferred_element_type=jnp.float32)
    if final_norm:
        ms = jnp.mean(y * y, axis=-1, keepdims=True)
        y = y * lax.rsqrt(ms + RMS_EPS) * g_ref[...]
    o_ref[...] = y


def _out_proj(merged, w_out, x, final_gain, final_norm):
    B, L, D = x.shape
    tm = MERGE_TM
    tok = pl.BlockSpec((None, tm, D), lambda b, i: (b, i, 0))
    return pl.pallas_call(
        functools.partial(_out_kernel, final_norm=final_norm),
        grid=(B, L // tm),
        in_specs=[tok, pl.BlockSpec(w_out.shape, lambda b, i: (0, 0), pipeline_mode=pl.Buffered(1)),
                  tok, pl.BlockSpec((1, D), lambda b, i: (0, 0))],
        out_specs=tok,
        out_shape=jax.ShapeDtypeStruct((B, L, D), jnp.float32),
        compiler_params=_cparams(2),
        name="out_proj",
    )(merged, w_out, x, final_gain)


def _prep_layer(i, norm_gain, w_in, conv_w, conv_b, filt_w1, filt_b1, filt_w2, filt_b2, filt_w3, filt_b3,
                filt_w4, filt_freq, hyena_skip, w_branch_attn, w_branch_hyena, w_out):
    bf = jnp.bfloat16
    return dict(
        gain=norm_gain[i][None, :],
        w_in=w_in[i].astype(bf),
        conv_w=conv_w[i], conv_b=conv_b[i][None, :],
        filt_w1=filt_w1[i], filt_b1=filt_b1[i], filt_w2=filt_w2[i], filt_b2=filt_b2[i],
        filt_w3=filt_w3[i], filt_b3=filt_b3[i], filt_w4=filt_w4[i], filt_freq=filt_freq[i],
        hyena_skip=hyena_skip[i],
        w_ba=w_branch_attn[i].astype(bf), w_bh=w_branch_hyena[i].astype(bf), w_out=w_out[i].astype(bf))


def _layer(x, lp, final_gain, final_norm):
    B, L, D = x.shape
    assert L % (2 * DFT_HALF) == 0 and L % (2 * PROJ_TM) == 0 and D == 2 * W_BLOCK
    n2 = L // DFT_HALF

    tabs = _rope_tables(L)
    outs, lses = [], []
    for g, (_, dil) in enumerate(ATTN_GROUPS):
        qkv = _proj_qkv(x, lp["gain"], lp["w_in"], tabs, dil, g)
        n = L // dil
        o, lse = _attention(qkv.reshape(3, B * dil, n, ATTN_WIDTH))
        outs.append(o.reshape(B, dil, n, ATTN_WIDTH))
        lses.append(lse.reshape(B, dil, n, LANES))

    gate_cols = (lambda s: jnp.where(s == 0, GATE_BLOCKS[0], GATE_BLOCKS[2] + 2 * (s - 1)),
                 lambda s: jnp.where(s == 0, GATE_BLOCKS[1], GATE_BLOCKS[2] + 2 * (s - 1) + 1))
    gates = _proj_plain(x, lp["gain"], lp["w_in"], gate_cols, 3, jnp.bfloat16)
    uhy_cols = tuple(functools.partial(lambda s, c: c, c=UHY_BLOCK + k) for k in range(3))
    u_hy = _proj_plain(x, lp["gain"], lp["w_in"], uhy_cols, 1, jnp.bfloat16)
    gated, mult = _hyena_pre(u_hy, gates, lp["conv_w"], lp["conv_b"], n2)
    fwd_tabs, inv_tabs = _dft_tables(n2)
    kf = _filter_spectrum(_filter(L, n2, lp), fwd_tabs)
    hyena = _long_conv(gated, mult, fwd_tabs, inv_tabs, kf)

    merged = _merge(outs, lses, hyena, gates, lp["w_ba"], lp["w_bh"], n2)
    return _out_proj(merged, lp["w_out"], x, final_gain, final_norm)


def kernel(x_prompt, x_sample, norm_gain, w_in, conv_w, conv_b, filt_w1, filt_b1, filt_w2, filt_b2, filt_w3, filt_b3, filt_w4, filt_freq, hyena_skip, w_branch_attn, w_branch_hyena, w_out, final_gain):
    depth = w_in.shape[0]
    layers = [_prep_layer(i, norm_gain, w_in, conv_w, conv_b, filt_w1, filt_b1, filt_w2, filt_b2, filt_w3,
                          filt_b3, filt_w4, filt_freq, hyena_skip, w_branch_attn, w_branch_hyena, w_out)
              for i in range(depth)]
    fg = final_gain[None, :]

    def trunk(x):
        for i, lp in enumerate(layers):
            x = _layer(x, lp, fg, final_norm=(i == depth - 1))
        return x

    return (trunk(x_prompt), trunk(x_sample))
```

```python
import functools
import math

import jax
import jax.numpy as jnp
import numpy as np
from jax import lax
from jax.experimental import pallas as pl
from jax.experimental.pallas import tpu as pltpu

HEAD_DIM = 128
N_SLOTS = 8
ATTN_GROUPS = ((128, 1), (512, 4), (2048, 16))
N_GROUPS = len(ATTN_GROUPS)
ATTN_WIDTH = N_SLOTS * HEAD_DIM
QKV_WIDTH = N_GROUPS * ATTN_WIDTH
ROT_DIM = HEAD_DIM // 4
ROPE_THETA = 500000.0
HYENA_WIDTH = 1024
FILTER_EMB = 33
FILTER_HIDDEN = 64
DECAY_TARGET = 1e-2
DECAY_FAST_PCT = 0.3
DECAY_SLOW_PCT = 1.5
RMS_EPS = 1e-6
RADIUS = 64

LANES = 128
SUBLANES = 8
VMEM_LIMIT = 56 * 1024 * 1024
LONG_CONV_VMEM = 36 * 1024 * 1024
LONG_CONV_RESIDENT_VMEM = 48 * 1024 * 1024

DFT_N1 = 512
DFT_HALF = DFT_N1 // 2
DFT_ROWS = 264
DFT_M = 2 * DFT_ROWS

PROJ_TM = 512
W_BLOCK = 1024
_ZA_BLOCK = 3 * QKV_WIDTH // W_BLOCK
UHY_BLOCK = _ZA_BLOCK + ATTN_WIDTH // W_BLOCK
_ZH_BLOCK = UHY_BLOCK + 3 * HYENA_WIDTH // W_BLOCK
GATE_BLOCKS = (_ZA_BLOCK, _ZH_BLOCK, _ZH_BLOCK + HYENA_WIDTH // W_BLOCK)
PERM_MIN_DIL = 8
ATTN_TQ = 512
ATTN_SUB = 128
HY_TL = 512
BF16_TILE_ROWS = 2 * SUBLANES
HALO_ROWS = BF16_TILE_ROWS
MERGE_TM = 512


def _cparams(n_axes):
    return pltpu.CompilerParams(dimension_semantics=("parallel",) * n_axes,
                                vmem_limit_bytes=VMEM_LIMIT)


def _sigmoid(x):
    return 0.5 * jnp.tanh(0.5 * x) + 0.5


def _split_bf16(x):
    hi = x.astype(jnp.bfloat16)
    lo = (x - hi.astype(jnp.float32)).astype(jnp.bfloat16)
    return hi, lo


def _normed(x_ref, g_ref, rows=slice(None)):
    x = x_ref[rows, :]
    ms = jnp.mean(x * x, axis=-1, keepdims=True)
    return (x * lax.rsqrt(ms + RMS_EPS) * g_ref[...]).astype(jnp.bfloat16)


def _proj_plain_kernel(x_ref, g_ref, *rest):
    w_refs, o_ref = rest[:-1], rest[-1]
    tm = x_ref.shape[0]
    for half in range(2):
        rows = slice(half * tm // 2, (half + 1) * tm // 2)
        h = _normed(x_ref, g_ref, rows)
        for k, w_ref in enumerate(w_refs):
            cw = w_ref.shape[1]
            o_ref[rows, k * cw:(k + 1) * cw] = jnp.dot(
                h, w_ref[...], preferred_element_type=jnp.float32).astype(o_ref.dtype)


def _proj_plain(x, gain, wb, col_maps, n_slabs, out_dtype):
    B, L, D = x.shape
    tm = 2 * PROJ_TM
    cs = len(col_maps) * W_BLOCK
    w_specs = [pl.BlockSpec((D, W_BLOCK), functools.partial(lambda s, b, i, f: (0, f(s)), f=f),
                            pipeline_mode=pl.Buffered(1)) for f in col_maps]
    return pl.pallas_call(
        _proj_plain_kernel,
        grid=(n_slabs, B, L // tm),
        in_specs=[pl.BlockSpec((None, tm, D), lambda s, b, i: (b, i, 0)),
                  pl.BlockSpec((1, D), lambda s, b, i: (0, 0))] + w_specs,
        out_specs=pl.BlockSpec((None, tm, cs), lambda s, b, i: (b, i, s)),
        out_shape=jax.ShapeDtypeStruct((B, L, n_slabs * cs), out_dtype),
        compiler_params=_cparams(3),
        name="proj_plain",
    )(x, gain, *([wb] * len(col_maps)))


def _proj_qkv_kernel(x_ref, g_ref, wq_ref, wk_ref, wv_ref, cq_ref, sq_ref, ck_ref, sk_ref, *rest, dil, perm):
    tm = PROJ_TM
    rows = tm // dil
    if perm:
        p_ref, o_ref = rest
    elif dil > 1:
        o_ref, scr = rest
    else:
        (o_ref,) = rest
    lane = lax.broadcasted_iota(jnp.int32, (tm, HEAD_DIM), 1)
    first_half = lane < (ROT_DIM // 2)
    for half in range(x_ref.shape[0] // tm):
        tok = slice(half * tm, (half + 1) * tm)
        h = _normed(x_ref, g_ref, tok)
        if perm:
            h = jnp.dot(p_ref[...], h, preferred_element_type=jnp.float32).astype(jnp.bfloat16)
        for which, w_ref in enumerate((wq_ref, wk_ref, wv_ref)):
            acc = jnp.dot(h, w_ref[...], preferred_element_type=jnp.float32)
            for slot in range(N_SLOTS):
                cs = slice(slot * HEAD_DIM, (slot + 1) * HEAD_DIM)
                t = acc[:, cs]
                if which < 2:
                    cos_ref, sin_ref = (cq_ref, sq_ref) if which == 0 else (ck_ref, sk_ref)
                    partner = jnp.where(first_half,
                                        pltpu.roll(t, HEAD_DIM - ROT_DIM // 2, axis=1),
                                        pltpu.roll(t, ROT_DIM // 2, axis=1))
                    t = t * cos_ref[tok, :] + partner * sin_ref[tok, :]
                if perm or dil == 1:
                    for r in range(dil):
                        o_ref[which, r, half * rows:(half + 1) * rows, cs] = (
                            t[r * rows:(r + 1) * rows].astype(o_ref.dtype))
                else:
                    scr[slot] = t
            if not perm and dil > 1:
                for slot in range(N_SLOTS):
                    for r in range(dil):
                        o_ref[which, r, half * rows:(half + 1) * rows, slot * HEAD_DIM:(slot + 1) * HEAD_DIM] = (
                            scr[slot, pl.ds(r, rows, stride=dil), :].astype(o_ref.dtype))


def _class_major(a, tm, dil):
    n_tiles = a.shape[0] // tm
    return a.reshape(n_tiles, tm // dil, dil, -1).transpose(0, 2, 1, 3).reshape(a.shape)


def _class_major_onehot(tm, dil):
    src = np.arange(tm).reshape(tm // dil, dil).T.reshape(tm)
    onehot = np.zeros((tm, tm), np.float32)
    onehot[np.arange(tm), src] = 1.0
    return jnp.asarray(onehot, jnp.bfloat16)


def _proj_qkv(x, gain, wb, tabs, dil, group):
    B, L, D = x.shape
    half = PROJ_TM
    tm = 2 * half
    n = L // dil
    perm = dil >= PERM_MIN_DIL
    tab_spec = pl.BlockSpec((tm, HEAD_DIM), lambda b, i: (i, 0))
    w_specs = [pl.BlockSpec((D, W_BLOCK), functools.partial(lambda b, i, c: (0, c), c=t * N_GROUPS + group),
                            pipeline_mode=pl.Buffered(1)) for t in range(3)]
    in_specs = [pl.BlockSpec((None, tm, D), lambda b, i: (b, i, 0)),
                pl.BlockSpec((1, D), lambda b, i: (0, 0))] + w_specs + [tab_spec] * 4
    args = [x, gain, wb, wb, wb]
    scratch = []
    if perm:
        args += [_class_major(t, half, dil) for t in tabs] + [_class_major_onehot(half, dil)]
        in_specs.append(pl.BlockSpec((half, half), lambda b, i: (0, 0), pipeline_mode=pl.Buffered(1)))
    else:
        args += list(tabs)
        if dil > 1:
            scratch = [pltpu.VMEM((N_SLOTS, half, HEAD_DIM), jnp.float32)]
    return pl.pallas_call(
        functools.partial(_proj_qkv_kernel, dil=dil, perm=perm),
        grid=(B, L // tm),
        in_specs=in_specs,
        out_specs=pl.BlockSpec((3, None, dil, tm // dil, ATTN_WIDTH), lambda b, i: (0, b, 0, i, 0)),
        out_shape=jax.ShapeDtypeStruct((3, B, dil, n, ATTN_WIDTH), jnp.bfloat16),
        scratch_shapes=scratch,
        compiler_params=_cparams(2),
        name=f"proj_qkv_d{dil}",
    )(*args)


def _rope_tables(L):
    inv_freq = jnp.power(ROPE_THETA, -jnp.arange(0, ROT_DIM, 2, dtype=jnp.float32) / ROT_DIM)
    ang = jnp.arange(L, dtype=jnp.float32)[:, None] * inv_freq[None, :]
    cos, sin = jnp.cos(ang), jnp.sin(ang)
    ones = jnp.ones((L, HEAD_DIM - ROT_DIM), jnp.float32)
    cos_t = jnp.concatenate([cos, cos, ones], axis=1)
    sin_t = jnp.concatenate([-sin, sin, 0.0 * ones], axis=1)
    scale = 1.0 / math.sqrt(HEAD_DIM)
    return cos_t * scale, sin_t * scale, cos_t, sin_t


def _attn_kernel(q_ref, kp_ref, kc_ref, kn_ref, vp_ref, vc_ref, vn_ref, o_ref, lse_ref, kw, vw, *, n):
    tq = q_ref.shape[0]
    i = pl.program_id(1)
    kw[0:RADIUS] = kp_ref[...]
    kw[RADIUS:RADIUS + tq] = kc_ref[...]
    kw[RADIUS + tq:] = kn_ref[...]
    vw[0:RADIUS] = vp_ref[...]
    vw[RADIUS:RADIUS + tq] = vc_ref[...]
    vw[RADIUS + tq:] = vn_ref[...]
    win = ATTN_SUB + 2 * RADIUS
    row = lax.broadcasted_iota(jnp.int32, (ATTN_SUB, win), 0)
    col = lax.broadcasted_iota(jnp.int32, (ATTN_SUB, win), 1)
    band = (col >= row) & (col <= row + 2 * RADIUS)
    lane = lax.broadcasted_iota(jnp.int32, (ATTN_SUB, LANES), 1)
    nsub = tq // ATTN_SUB
    for s in range(nsub):
        q0 = i * tq + s * ATTN_SUB
        valid = band
        if s == 0:
            valid = valid & (col >= RADIUS - q0)
        if s == nsub - 1:
            valid = valid & (col < n - q0 + RADIUS)
        lse_tile = jnp.zeros((ATTN_SUB, LANES), jnp.float32)
        for h in range(N_SLOTS):
            cs = slice(h * HEAD_DIM, (h + 1) * HEAD_DIM)
            qb = q_ref[s * ATTN_SUB:(s + 1) * ATTN_SUB, cs]
            kb = kw[s * ATTN_SUB:s * ATTN_SUB + win, cs]
            vb = vw[s * ATTN_SUB:s * ATTN_SUB + win, cs]
            sc = lax.dot_general(qb, kb, (((1,), (1,)), ((), ())), preferred_element_type=jnp.float32)
            sc = jnp.where(valid, sc, -1e30)
            m = jnp.max(sc, axis=1, keepdims=True)
            p = jnp.exp(sc - m)
            l = jnp.sum(p, axis=1, keepdims=True)
            o = jnp.dot(p.astype(jnp.bfloat16), vb, preferred_element_type=jnp.float32)
            o_ref[s * ATTN_SUB:(s + 1) * ATTN_SUB, cs] = o / l
            lse_tile = jnp.where(lane == h, m + jnp.log(l), lse_tile)
        lse_ref[s * ATTN_SUB:(s + 1) * ATTN_SUB, :] = lse_tile


def _attention(qkv):
    _, S, n, W = qkv.shape
    tq = min(ATTN_TQ, n)
    r = tq // RADIUS
    last = n // RADIUS - 1

    def cur(which):
        return pl.BlockSpec((None, None, tq, W), lambda s, i: (which, s, i, 0))

    def prev(which):
        return pl.BlockSpec((None, None, RADIUS, W), lambda s, i: (which, s, jnp.maximum(i * r - 1, 0), 0))

    def nxt(which):
        return pl.BlockSpec((None, None, RADIUS, W), lambda s, i: (which, s, jnp.minimum((i + 1) * r, last), 0))

    return pl.pallas_call(
        functools.partial(_attn_kernel, n=n),
        grid=(S, n // tq),
        in_specs=[cur(0), prev(1), cur(1), nxt(1), prev(2), cur(2), nxt(2)],
        out_specs=[pl.BlockSpec((None, tq, W), lambda s, i: (s, i, 0)),
                   pl.BlockSpec((None, tq, LANES), lambda s, i: (s, i, 0))],
        out_shape=[jax.ShapeDtypeStruct((S, n, W), jnp.float32),
                   jax.ShapeDtypeStruct((S, n, LANES), jnp.float32)],
        scratch_shapes=[pltpu.VMEM((tq + 2 * RADIUS, W), jnp.bfloat16),
                        pltpu.VMEM((tq + 2 * RADIUS, W), jnp.bfloat16)],
        compiler_params=_cparams(2),
        name="attn",
    )(qkv, qkv, qkv, qkv, qkv, qkv, qkv)


def _shift_rows(u, edge_row, down):
    tl = u.shape[0]
    rolled = pltpu.roll(u, 1 if down else tl - 1, axis=0)
    r8 = lax.broadcasted_iota(jnp.int32, (SUBLANES, u.shape[1]), 0)
    if down:
        head = jnp.where(r8 == 0, edge_row, rolled[:SUBLANES])
        return jnp.concatenate([head, rolled[SUBLANES:]], axis=0)
    tail = jnp.where(r8 == SUBLANES - 1, edge_row, rolled[tl - SUBLANES:])
    return jnp.concatenate([rolled[:tl - SUBLANES], tail], axis=0)


def _hyena_pre_kernel(u_ref, up_ref, un_ref, z_ref, cw_ref, cb_ref, p_ref, g_ref, m_ref, *, n2, nblk):
    tl = u_ref.shape[0]
    i = pl.program_id(1)
    has_prev = (i > 0).astype(jnp.float32)
    has_next = (i < nblk - 1).astype(jnp.float32)
    C = HYENA_WIDTH
    rows = tl // n2
    perm = p_ref[...]

    def conv(cs):
        u = u_ref[:, cs].astype(jnp.float32)
        edge_prev = up_ref[HALO_ROWS - 1:HALO_ROWS, cs].astype(jnp.float32) * has_prev
        edge_next = un_ref[0:1, cs].astype(jnp.float32) * has_next
        u_prev = _shift_rows(u, edge_prev, True)
        u_next = _shift_rows(u, edge_next, False)
        return u_prev * cw_ref[0:1, cs] + u * cw_ref[1:2, cs] + u_next * cw_ref[2:3, cs] + cb_ref[:, cs]

    for c in range(C // LANES):
        ls = slice(c * LANES, (c + 1) * LANES)
        x0, x1, vh = (conv(slice(k * C + c * LANES, k * C + (c + 1) * LANES)) for k in range(3))
        z = z_ref[:, ls].astype(jnp.float32)
        m_hi, m_lo = _split_bf16(x0 * (z * _sigmoid(z)))
        stack = jnp.concatenate([(vh * x1).astype(jnp.bfloat16), m_hi, m_lo], axis=1)
        res = jnp.dot(perm, stack, preferred_element_type=jnp.float32)
        g = res[:, :LANES]
        m = res[:, LANES:2 * LANES] + res[:, 2 * LANES:]
        for r in range(n2):
            g_ref[r, :, ls] = g[r * rows:(r + 1) * rows].astype(g_ref.dtype)
            m_ref[r, :, ls] = m[r * rows:(r + 1) * rows]


def _hyena_pre(u_hy, gates, conv_w, conv_b, n2):
    B, L, C3 = u_hy.shape
    C = HYENA_WIDTH
    tl = HY_TL
    nblk = L // tl
    per = tl // HALO_ROWS
    last_halo = L // HALO_ROWS - 1
    out_spec = pl.BlockSpec((None, n2, tl // n2, C), lambda b, i: (b, 0, i, 0))
    out_sds = jax.ShapeDtypeStruct((B, n2, L // n2, C), jnp.float32)
    g_sds = jax.ShapeDtypeStruct(out_sds.shape, jnp.bfloat16 if (tl // n2) % BF16_TILE_ROWS == 0 else jnp.float32)
    return pl.pallas_call(
        functools.partial(_hyena_pre_kernel, n2=n2, nblk=nblk),
        grid=(B, nblk),
        in_specs=[pl.BlockSpec((None, tl, C3), lambda b, i: (b, i, 0)),
                  pl.BlockSpec((None, HALO_ROWS, C3), lambda b, i: (b, jnp.maximum(i * per - 1, 0), 0)),
                  pl.BlockSpec((None, HALO_ROWS, C3), lambda b, i: (b, jnp.minimum((i + 1) * per, last_halo), 0)),
                  pl.BlockSpec((None, tl, C), lambda b, i: (b, i, 1)),
                  pl.BlockSpec((3, C3), lambda b, i: (0, 0)),
                  pl.BlockSpec((1, C3), lambda b, i: (0, 0)),
                  pl.BlockSpec((tl, tl), lambda b, i: (0, 0), pipeline_mode=pl.Buffered(1))],
        out_specs=[out_spec, out_spec],
        out_shape=[g_sds, out_sds],
        compiler_params=_cparams(2),
        name="hyena_pre",
    )(u_hy, u_hy, u_hy, gates, conv_w, conv_b, _class_major_onehot(tl, n2))


POS_VALID, POS_ZERO = FILTER_EMB, FILTER_EMB + 1


def _dot3(a, wh_ref, wl_ref):
    ah, al = _split_bf16(a)
    wh = wh_ref[...]
    return (jnp.dot(ah, wh, preferred_element_type=jnp.float32)
            + jnp.dot(al, wh, preferred_element_type=jnp.float32)
            + jnp.dot(ah, wl_ref[...], preferred_element_type=jnp.float32))


def _filter_kernel(p_ref, w1h, w1l, b1_ref, w2h, w2l, b2_ref, w3h, w3l, b3_ref, w4fh, w4fl, w4bh, w4bl,
                   fr_ref, dl_ref, sk_ref, o_ref):
    p = p_ref[...]
    fr = fr_ref[...]
    p2 = jnp.concatenate([p[:DFT_HALF], p[DFT_HALF:]], axis=1)
    hdn = jnp.sin(fr * (_dot3(p2, w1h, w1l) + b1_ref[...]))
    hdn = jnp.sin(fr * (_dot3(hdn, w2h, w2l) + b2_ref[...]))
    hdn = jnp.sin(fr * (_dot3(hdn, w3h, w3l) + b3_ref[...]))
    scale = jnp.exp(-p[:, 0:1] * dl_ref[...]) * p[:, POS_VALID:POS_VALID + 1]
    skip = p[:, POS_ZERO:POS_ZERO + 1] * sk_ref[...]
    for d, (wh, wl) in enumerate(((w4fh, w4fl), (w4bh, w4bl))):
        rs = slice(d * DFT_HALF, (d + 1) * DFT_HALF)
        o_ref[rs, :] = (_dot3(hdn, wh, wl) * scale[rs] + skip[rs]).astype(o_ref.dtype)


def _filter_lag_table(L, n2):
    bands = (FILTER_EMB - 1) // 2
    f32 = np.float32
    n = np.arange(2 * L).reshape(2 * L // n2, n2).T.reshape(2 * L)
    lag = np.where(n < L, n, 2 * L - n)
    idx = np.where(n == L, 0, lag).astype(f32)[:, None]
    t = np.where(idx == L - 1, f32(1.0), idx * f32(1.0 / (L - 1))).astype(f32)
    w = (f32(2.0 * math.pi) * idx / f32(L)).astype(f32)
    f = np.linspace(1e-4, bands - 1, bands, dtype=f32)[None, :]
    arg = (f * w).astype(f32)
    flags = np.stack([n != L, n == 0], axis=1).astype(f32)
    tab = np.concatenate([t, np.cos(arg), -np.sin(arg), flags], axis=1).astype(f32)
    return jnp.pad(jnp.asarray(tab), ((0, 0), (0, LANES - tab.shape[1])))


def _filter(L, n2, lp):
    C = HYENA_WIDTH
    tab = _filter_lag_table(L, n2)
    H = FILTER_HIDDEN
    assert 2 * H == LANES
    zeros = lambda r, c: jnp.zeros((r, c), jnp.float32)
    blockdiag = lambda w: jnp.block([[w, zeros(*w.shape)], [zeros(*w.shape), w]])
    w1p = jnp.concatenate([lp["filt_w1"], zeros(LANES - FILTER_EMB, H)], axis=0)
    w4 = lp["filt_w4"]
    w4f = jnp.concatenate([w4[:, :C], zeros(H, C)], axis=0)
    w4b = jnp.concatenate([zeros(H, C), w4[:, C:]], axis=0)
    twice = lambda v: jnp.concatenate([v, v])[None, :]
    min_decay = math.log(DECAY_TARGET) / DECAY_FAST_PCT
    max_decay = math.log(DECAY_TARGET) / DECAY_SLOW_PCT
    deltas = jnp.abs(jnp.linspace(min_decay, max_decay, C, dtype=jnp.float32))[None, :]
    full = lambda a: pl.BlockSpec(a.shape, lambda i: (0,) * a.ndim)
    args = [*_split_bf16(blockdiag(w1p)), twice(lp["filt_b1"]), *_split_bf16(blockdiag(lp["filt_w2"])),
            twice(lp["filt_b2"]), *_split_bf16(blockdiag(lp["filt_w3"])), twice(lp["filt_b3"]),
            *_split_bf16(w4f), *_split_bf16(w4b), twice(lp["filt_freq"]), deltas, lp["hyena_skip"][None, :]]
    k = pl.pallas_call(
        _filter_kernel,
        grid=(n2,),
        in_specs=[pl.BlockSpec((DFT_N1, LANES), lambda i: (i, 0))] + [full(a) for a in args],
        out_specs=pl.BlockSpec((DFT_N1, C), lambda i: (i, 0)),
        out_shape=jax.ShapeDtypeStruct((2 * L, C), jnp.bfloat16),
        compiler_params=_cparams(1),
        name="filter",
    )(tab, *args)
    return k.reshape(n2, DFT_N1, C)


def _dft_tables(n2):
    N = DFT_N1 * n2
    k1 = np.arange(DFT_ROWS)
    live = (k1 <= DFT_HALF).astype(np.float64)[:, None]
    a = 2.0 * np.pi * ((k1[:, None] * np.arange(DFT_N1)[None, :]) % DFT_N1) / DFT_N1
    b = 2.0 * np.pi * (k1[:, None] * np.arange(n2)[None, :]) / N
    fwd = np.concatenate([np.cos(a) * live, -np.sin(a) * live], axis=0)
    weight = (np.where((k1 == 0) | (k1 == DFT_HALF), 1.0, 2.0) / N)[:, None]
    inv = np.concatenate([np.cos(a) * live * weight, -np.sin(a) * live * weight], axis=0).T[:DFT_HALF]
    tw = np.stack([np.cos(b), np.sin(b)], axis=2).reshape(DFT_ROWS // SUBLANES, SUBLANES, 2 * n2)
    return jnp.asarray(fwd, jnp.bfloat16), jnp.asarray(inv, jnp.bfloat16), jnp.asarray(tw, jnp.float32)


def _twiddle_factors(tw_c, n2):
    bc = lambda col: jnp.broadcast_to(tw_c[:, col:col + 1], (SUBLANES, LANES))
    return [None if r == 0 else (bc(2 * r), bc(2 * r + 1)) for r in range(n2)]


def _twiddle(z, factors, sign):
    if factors is None:
        return z
    re, im = z
    cb, sb = factors
    if sign < 0:
        return re * cb + im * sb, im * cb - re * sb
    return re * cb - im * sb, re * sb + im * cb


def _filter_spectrum_kernel(k_ref, f_ref, tw_ref, o_ref, *, n2, slabs):
    nj = n2 // slabs
    j = pl.program_id(1)
    for s in range(slabs):
        o_ref[j * slabs + s] = jnp.dot(f_ref[...], k_ref[s], preferred_element_type=jnp.float32)

    @pl.when(j == nj - 1)
    def _across_slabs():
        def body(c, carry):
            re = pl.ds(pl.multiple_of(c * SUBLANES, SUBLANES), SUBLANES)
            im = pl.ds(pl.multiple_of(c * SUBLANES + DFT_ROWS, SUBLANES), SUBLANES)
            tw = _twiddle_factors(tw_ref[c], n2)
            for t in range(o_ref.shape[2] // LANES):
                ls = slice(t * LANES, (t + 1) * LANES)
                ys = _fft_list([_twiddle((o_ref[r, re, ls], o_ref[r, im, ls]), tw[r], -1)
                                for r in range(n2)], -1)
                for k2 in range(n2):
                    o_ref[k2, re, ls] = ys[k2][0]
                    o_ref[k2, im, ls] = ys[k2][1]
            return carry

        lax.fori_loop(0, DFT_ROWS // SUBLANES, body, 0, unroll=3)


def _const_spec(a):
    return pl.BlockSpec(a.shape, lambda *_: (0,) * a.ndim, pipeline_mode=pl.Buffered(1))


def _filter_spectrum(k, fwd, tw, slabs=8):
    n2, K, C = k.shape
    slabs = min(slabs, n2)
    tc = 2 * LANES
    return pl.pallas_call(
        functools.partial(_filter_spectrum_kernel, n2=n2, slabs=slabs),
        grid=(C // tc, n2 // slabs),
        in_specs=[pl.BlockSpec((slabs, K, tc), lambda c, j: (j, 0, c)), _const_spec(fwd), _const_spec(tw)],
        out_specs=pl.BlockSpec((n2, DFT_M, tc), lambda c, j: (0, 0, c), pipeline_mode=pl.Buffered(1)),
        out_shape=jax.ShapeDtypeStruct((n2, DFT_M, C), jnp.float32),
        compiler_params=pltpu.CompilerParams(dimension_semantics=("parallel", "arbitrary"),
                                             vmem_limit_bytes=VMEM_LIMIT),
        name="filter_spectrum",
    )(k, fwd, tw)


def _fft_list(xs, sign):
    n = len(xs)
    if n == 1:
        return xs
    even = _fft_list(xs[0::2], sign)
    odd = _fft_list(xs[1::2], sign)
    out = [None] * n
    for k in range(n // 2):
        er, ei = even[k]
        pr, pi = odd[k]
        if k == 0:
            tr, ti = pr, pi
        elif 4 * k == n:
            tr, ti = (pi, -pr) if sign < 0 else (-pi, pr)
        else:
            ang = sign * 2.0 * math.pi * k / n
            wr, wi = math.cos(ang), math.sin(ang)
            tr, ti = pr * wr - pi * wi, pr * wi + pi * wr
        out[k] = (er + tr, ei + ti)
        out[k + n // 2] = (er - tr, ei - ti)
    return out


def _long_conv_kernel(x_ref, f_ref, g_ref, tw_ref, gate_ref, kf_ref, o_ref, spec, *, n2, slabs, resident):
    nj = n2 // slabs
    j = pl.program_id(2)
    tc = spec.shape[2]
    fwd_base = j * slabs if resident else 0
    inv_base = (j - nj) * slabs if resident else 0

    @pl.when(j < nj)
    def _forward():
        for s in range(slabs):
            spec[j * slabs + s] = jnp.dot(f_ref[...], x_ref[fwd_base + s].astype(jnp.bfloat16),
                                          preferred_element_type=jnp.float32)

    @pl.when(j == nj - 1)
    def _across_slabs():
        def body(c, carry):
            re = pl.ds(pl.multiple_of(c * SUBLANES, SUBLANES), SUBLANES)
            im = pl.ds(pl.multiple_of(c * SUBLANES + DFT_ROWS, SUBLANES), SUBLANES)
            tw = _twiddle_factors(tw_ref[c], n2)
            for t in range(tc // LANES):
                ls = slice(t * LANES, (t + 1) * LANES)
                ys = _fft_list([_twiddle((spec[r, re, ls], spec[r, im, ls]), tw[r], -1)
                                for r in range(n2)], -1)
                prod = []
                for k2 in range(n2):
                    yr, yi = ys[k2]
                    fr, fi = kf_ref[k2, re, ls], kf_ref[k2, im, ls]
                    prod.append((yr * fr - yi * fi, yr * fi + yi * fr))
                vs = _fft_list(prod, +1)
                for r in range(n2):
                    vr, vi = _twiddle(vs[r], tw[r], +1)
                    spec[r, re, ls] = vr
                    spec[r, im, ls] = vi
            return carry

        lax.fori_loop(0, DFT_ROWS // SUBLANES, body, 0, unroll=3)

    @pl.when(j >= nj)
    def _inverse():
        for s in range(slabs):
            v = spec[(j - nj) * slabs + s].astype(jnp.bfloat16)
            o_ref[inv_base + s] = (jnp.dot(g_ref[...], v, preferred_element_type=jnp.float32)
                                   * gate_ref[inv_base + s])


def _long_conv(x, gate, fwd, inv, tw, kf, slabs=8):
    B, n2, K, C = x.shape
    slabs = min(slabs, n2)
    nj = n2 // slabs
    spectra = lambda tc: 2 * n2 * DFT_M * tc * 4
    tc = 2 * LANES if spectra(2 * LANES) <= LONG_CONV_VMEM else LANES
    whole = 2 * n2 * K * tc * (4 + 4 + x.dtype.itemsize)
    resident = spectra(tc) + whole <= LONG_CONV_RESIDENT_VMEM
    fwd_j = lambda j: jnp.minimum(j, nj - 1)
    inv_j = lambda j: jnp.maximum(j - nj, 0)
    if resident:
        io_spec = pl.BlockSpec((None, n2, K, tc), lambda c, b, j: (b, 0, 0, c))
        x_spec = io_spec
    else:
        io_spec = pl.BlockSpec((None, slabs, K, tc), lambda c, b, j: (b, inv_j(j), 0, c))
        x_spec = pl.BlockSpec((None, slabs, K, tc), lambda c, b, j: (b, fwd_j(j), 0, c))
    f_spec = pl.BlockSpec((DFT_M, K), lambda c, b, j: (0, 0), pipeline_mode=pl.Buffered(1))
    return pl.pallas_call(
        functools.partial(_long_conv_kernel, n2=n2, slabs=slabs, resident=resident),
        grid=(C // tc, B, 2 * nj),
        in_specs=[x_spec, f_spec, _const_spec(inv), _const_spec(tw), io_spec,
                  pl.BlockSpec((n2, DFT_M, tc), lambda c, b, j: (0, 0, c), pipeline_mode=pl.Buffered(1))],
        out_specs=io_spec,
        out_shape=jax.ShapeDtypeStruct((B, n2, K, C), jnp.float32),
        scratch_shapes=[pltpu.VMEM((n2, DFT_M, tc), jnp.float32)],
        compiler_params=pltpu.CompilerParams(dimension_semantics=("parallel", "parallel", "arbitrary"),
                                             vmem_limit_bytes=VMEM_LIMIT),
        name="long_conv",
    )(x, fwd, inv, tw, gate, kf)


def _regroup_to_tokens(blk_ref, scr, dil, width):
    m = blk_ref.shape[1]
    for c in range(width // LANES):
        for r in range(dil):
            scr[c, pl.ds(r, m, stride=dil), :] = blk_ref[r, :, c * LANES:(c + 1) * LANES]


def _merge_kernel(o0_ref, l0_ref, o1_ref, l1_ref, o2_ref, l2_ref, hy_ref, za_ref, ga_ref, gh_ref,
                  wa_ref, wh_ref, out_ref, so1, so2, sl1, sl2, shy, attn_scr, hy_scr, *, n2):
    d1, d2 = ATTN_GROUPS[1][1], ATTN_GROUPS[2][1]
    _regroup_to_tokens(o1_ref, so1, d1, ATTN_WIDTH)
    _regroup_to_tokens(o2_ref, so2, d2, ATTN_WIDTH)
    _regroup_to_tokens(l1_ref, sl1, d1, LANES)
    _regroup_to_tokens(l2_ref, sl2, d2, LANES)
    _regroup_to_tokens(hy_ref, shy, n2, HYENA_WIDTH)
    tm = out_ref.shape[0]
    for half in range(2):
        rs = slice(half * tm // 2, (half + 1) * tm // 2)
        l0, l1, l2 = l0_ref[rs, :], sl1[0, rs, :], sl2[0, rs, :]
        mx = jnp.maximum(jnp.maximum(l0, l1), l2)
        e0, e1, e2 = jnp.exp(l0 - mx), jnp.exp(l1 - mx), jnp.exp(l2 - mx)
        inv = 1.0 / (e0 + e1 + e2)
        a0, a1, a2 = e0 * inv, e1 * inv, e2 * inv
        for h in range(N_SLOTS):
            cs = slice(h * HEAD_DIM, (h + 1) * HEAD_DIM)
            mix = (a0[:, h:h + 1] * o0_ref[rs, cs] + a1[:, h:h + 1] * so1[h, rs, :]
                   + a2[:, h:h + 1] * so2[h, rs, :])
            z = za_ref[rs, cs].astype(jnp.float32)
            attn_scr[rs, cs] = (mix * (z * _sigmoid(z))).astype(jnp.bfloat16)
            hy_scr[rs, cs] = shy[h, rs, :].astype(jnp.bfloat16)
        br_a = jnp.dot(attn_scr[rs, :], wa_ref[...], preferred_element_type=jnp.float32)
        br_h = jnp.dot(hy_scr[rs, :], wh_ref[...], preferred_element_type=jnp.float32)
        merged = (_sigmoid(ga_ref[rs, :].astype(jnp.float32)) * br_a
                  + _sigmoid(gh_ref[rs, :].astype(jnp.float32)) * br_h)
        out_ref[rs, :] = merged.astype(out_ref.dtype)


def _merge(outs, lses, hyena, gates, w_ba, w_bh, n2):
    B, d0, L, W = outs[0].shape
    D = w_ba.shape[1]
    tm = MERGE_TM
    d1, d2 = ATTN_GROUPS[1][1], ATTN_GROUPS[2][1]

    def cls(d, width):
        return pl.BlockSpec((None, d, tm // d, width), lambda b, i: (b, 0, i, 0))

    in_specs = [pl.BlockSpec((None, None, tm, W), lambda b, i: (b, 0, i, 0)),
                pl.BlockSpec((None, None, tm, LANES), lambda b, i: (b, 0, i, 0)),
                cls(d1, W), cls(d1, LANES), cls(d2, W), cls(d2, LANES), cls(n2, HYENA_WIDTH),
                pl.BlockSpec((None, tm, W), lambda b, i: (b, i, 0)),
                pl.BlockSpec((None, tm, D), lambda b, i: (b, i, 1)),
                pl.BlockSpec((None, tm, D), lambda b, i: (b, i, 2)),
                pl.BlockSpec(w_ba.shape, lambda b, i: (0, 0), pipeline_mode=pl.Buffered(1)),
                pl.BlockSpec(w_bh.shape, lambda b, i: (0, 0), pipeline_mode=pl.Buffered(1))]
    wt = W // LANES
    return pl.pallas_call(
        functools.partial(_merge_kernel, n2=n2),
        grid=(B, L // tm),
        in_specs=in_specs,
        out_specs=pl.BlockSpec((None, tm, D), lambda b, i: (b, i, 0)),
        out_shape=jax.ShapeDtypeStruct((B, L, D), jnp.bfloat16),
        scratch_shapes=[pltpu.VMEM((wt, tm, LANES), jnp.float32), pltpu.VMEM((wt, tm, LANES), jnp.float32),
                        pltpu.VMEM((1, tm, LANES), jnp.float32), pltpu.VMEM((1, tm, LANES), jnp.float32),
                        pltpu.VMEM((wt, tm, LANES), jnp.float32),
                        pltpu.VMEM((tm, W), jnp.bfloat16), pltpu.VMEM((tm, HYENA_WIDTH), jnp.bfloat16)],
        compiler_params=_cparams(2),
        name="merge",
    )(outs[0], lses[0], outs[1], lses[1], outs[2], lses[2], hyena, gates, gates, gates, w_ba, w_bh)


def _out_kernel(m_ref, w_ref, x_ref, g_ref, o_ref, *, final_norm):
    y = x_ref[...] + jnp.dot(m_ref[...], w_ref[...], preferred_element_type=jnp.float32)
    if final_norm:
        ms = jnp.mean(y * y, axis=-1, keepdims=True)
        y = y * lax.rsqrt(ms + RMS_EPS) * g_ref[...]
    o_ref[...] = y


def _out_proj(merged, w_out, x, final_gain, final_norm):
    B, L, D = x.shape
    tm = MERGE_TM
    tok = pl.BlockSpec((None, tm, D), lambda b, i: (b, i, 0))
    return pl.pallas_call(
        functools.partial(_out_kernel, final_norm=final_norm),
        grid=(B, L // tm),
        in_specs=[tok, pl.BlockSpec(w_out.shape, lambda b, i: (0, 0), pipeline_mode=pl.Buffered(1)),
                  tok, pl.BlockSpec((1, D), lambda b, i: (0, 0))],
        out_specs=tok,
        out_shape=jax.ShapeDtypeStruct((B, L, D), jnp.float32),
        compiler_params=_cparams(2),
        name="out_proj",
    )(merged, w_out, x, final_gain)


def _prep_layer(i, norm_gain, w_in, conv_w, conv_b, filt_w1, filt_b1, filt_w2, filt_b2, filt_w3, filt_b3,
                filt_w4, filt_freq, hyena_skip, w_branch_attn, w_branch_hyena, w_out):
    bf = jnp.bfloat16
    return dict(
        gain=norm_gain[i][None, :],
        w_in=w_in[i].astype(bf),
        conv_w=conv_w[i], conv_b=conv_b[i][None, :],
        filt_w1=filt_w1[i], filt_b1=filt_b1[i], filt_w2=filt_w2[i], filt_b2=filt_b2[i],
        filt_w3=filt_w3[i], filt_b3=filt_b3[i], filt_w4=filt_w4[i], filt_freq=filt_freq[i],
        hyena_skip=hyena_skip[i],
        w_ba=w_branch_attn[i].astype(bf), w_bh=w_branch_hyena[i].astype(bf), w_out=w_out[i].astype(bf))


def _layer(x, lp, final_gain, final_norm):
    B, L, D = x.shape
    assert L % (2 * DFT_HALF) == 0 and L % (2 * PROJ_TM) == 0 and D == 2 * W_BLOCK
    n2 = L // DFT_HALF

    tabs = _rope_tables(L)
    outs, lses = [], []
    for g, (_, dil) in enumerate(ATTN_GROUPS):
        qkv = _proj_qkv(x, lp["gain"], lp["w_in"], tabs, dil, g)
        n = L // dil
        o, lse = _attention(qkv.reshape(3, B * dil, n, ATTN_WIDTH))
        outs.append(o.reshape(B, dil, n, ATTN_WIDTH))
        lses.append(lse.reshape(B, dil, n, LANES))

    gate_cols = (lambda s: jnp.where(s == 0, GATE_BLOCKS[0], GATE_BLOCKS[2] + 2 * (s - 1)),
                 lambda s: jnp.where(s == 0, GATE_BLOCKS[1], GATE_BLOCKS[2] + 2 * (s - 1) + 1))
    gates = _proj_plain(x, lp["gain"], lp["w_in"], gate_cols, 3, jnp.bfloat16)
    uhy_cols = tuple(functools.partial(lambda s, c: c, c=UHY_BLOCK + k) for k in range(3))
    u_hy = _proj_plain(x, lp["gain"], lp["w_in"], uhy_cols, 1, jnp.bfloat16)
    gated, mult = _hyena_pre(u_hy, gates, lp["conv_w"], lp["conv_b"], n2)
    fwd_mat, inv_mat, tw = _dft_tables(n2)
    kf = _filter_spectrum(_filter(L, n2, lp), fwd_mat, tw)
    hyena = _long_conv(gated, mult, fwd_mat, inv_mat, tw, kf)

    merged = _merge(outs, lses, hyena, gates, lp["w_ba"], lp["w_bh"], n2)
    return _out_proj(merged, lp["w_out"], x, final_gain, final_norm)


def kernel(x_prompt, x_sample, norm_gain, w_in, conv_w, conv_b, filt_w1, filt_b1, filt_w2, filt_b2, filt_w3, filt_b3, filt_w4, filt_freq, hyena_skip, w_branch_attn, w_branch_hyena, w_out, final_gain):
    depth = w_in.shape[0]
    layers = [_prep_layer(i, norm_gain, w_in, conv_w, conv_b, filt_w1, filt_b1, filt_w2, filt_b2, filt_w3,
                          filt_b3, filt_w4, filt_freq, hyena_skip, w_branch_attn, w_branch_hyena, w_out)
              for i in range(depth)]
    fg = final_gain[None, :]

    def trunk(x):
        for i, lp in enumerate(layers):
            x = _layer(x, lp, fg, final_norm=(i == depth - 1))
        return x

    return (trunk(x_prompt), trunk(x_sample))
```

```python
import functools
import math

import jax
import jax.numpy as jnp
import numpy as np
from jax import lax
from jax.experimental import pallas as pl
from jax.experimental.pallas import tpu as pltpu

HEAD_DIM = 128
N_SLOTS = 8
ATTN_GROUPS = ((128, 1), (512, 4), (2048, 16))
N_GROUPS = len(ATTN_GROUPS)
ATTN_WIDTH = N_SLOTS * HEAD_DIM
QKV_WIDTH = N_GROUPS * ATTN_WIDTH
ROT_DIM = HEAD_DIM // 4
ROPE_THETA = 500000.0
HYENA_WIDTH = 1024
FILTER_EMB = 33
FILTER_HIDDEN = 64
DECAY_TARGET = 1e-2
DECAY_FAST_PCT = 0.3
DECAY_SLOW_PCT = 1.5
RMS_EPS = 1e-6
RADIUS = 64

LANES = 128
SUBLANES = 8
VMEM_LIMIT = 56 * 1024 * 1024
LONG_CONV_VMEM = 36 * 1024 * 1024
LONG_CONV_RESIDENT_VMEM = 48 * 1024 * 1024

DFT_N1 = 512
DFT_HALF = DFT_N1 // 2
DFT_ROWS = 264
DFT_M = 2 * DFT_ROWS

PROJ_TM = 512
W_BLOCK = 1024
_ZA_BLOCK = 3 * QKV_WIDTH // W_BLOCK
UHY_BLOCK = _ZA_BLOCK + ATTN_WIDTH // W_BLOCK
_ZH_BLOCK = UHY_BLOCK + 3 * HYENA_WIDTH // W_BLOCK
GATE_BLOCKS = (_ZA_BLOCK, _ZH_BLOCK, _ZH_BLOCK + HYENA_WIDTH // W_BLOCK)
PERM_MIN_DIL = 8
ATTN_TQ = 512
ATTN_SUB = 128
HY_TL = 512
BF16_TILE_ROWS = 2 * SUBLANES
HALO_ROWS = BF16_TILE_ROWS
MERGE_TM = 512


def _cparams(n_axes):
    return pltpu.CompilerParams(dimension_semantics=("parallel",) * n_axes,
                                vmem_limit_bytes=VMEM_LIMIT)


def _sigmoid(x):
    return 0.5 * jnp.tanh(0.5 * x) + 0.5


def _split_bf16(x):
    hi = x.astype(jnp.bfloat16)
    lo = (x - hi.astype(jnp.float32)).astype(jnp.bfloat16)
    return hi, lo


def _normed(x_ref, g_ref, rows=slice(None)):
    x = x_ref[rows, :]
    ms = jnp.mean(x * x, axis=-1, keepdims=True)
    return (x * lax.rsqrt(ms + RMS_EPS) * g_ref[...]).astype(jnp.bfloat16)


def _proj_plain_kernel(x_ref, g_ref, *rest):
    w_refs, o_ref = rest[:-1], rest[-1]
    tm = x_ref.shape[0]
    for half in range(2):
        rows = slice(half * tm // 2, (half + 1) * tm // 2)
        h = _normed(x_ref, g_ref, rows)
        for k, w_ref in enumerate(w_refs):
            cw = w_ref.shape[1]
            o_ref[rows, k * cw:(k + 1) * cw] = jnp.dot(
                h, w_ref[...], preferred_element_type=jnp.float32).astype(o_ref.dtype)


def _proj_plain(x, gain, wb, col_maps, n_slabs, out_dtype):
    B, L, D = x.shape
    tm = 2 * PROJ_TM
    cs = len(col_maps) * W_BLOCK
    w_specs = [pl.BlockSpec((D, W_BLOCK), functools.partial(lambda s, b, i, f: (0, f(s)), f=f),
                            pipeline_mode=pl.Buffered(1)) for f in col_maps]
    return pl.pallas_call(
        _proj_plain_kernel,
        grid=(n_slabs, B, L // tm),
        in_specs=[pl.BlockSpec((None, tm, D), lambda s, b, i: (b, i, 0)),
                  pl.BlockSpec((1, D), lambda s, b, i: (0, 0))] + w_specs,
        out_specs=pl.BlockSpec((None, tm, cs), lambda s, b, i: (b, i, s)),
        out_shape=jax.ShapeDtypeStruct((B, L, n_slabs * cs), out_dtype),
        compiler_params=_cparams(3),
        name="proj_plain",
    )(x, gain, *([wb] * len(col_maps)))


def _proj_qkv_kernel(x_ref, g_ref, wq_ref, wk_ref, wv_ref, cq_ref, sq_ref, ck_ref, sk_ref, *rest, dil, perm):
    tm = PROJ_TM
    rows = tm // dil
    if perm:
        p_ref, o_ref = rest
    elif dil > 1:
        o_ref, scr = rest
    else:
        (o_ref,) = rest
    lane = lax.broadcasted_iota(jnp.int32, (tm, HEAD_DIM), 1)
    first_half = lane < (ROT_DIM // 2)
    for half in range(x_ref.shape[0] // tm):
        tok = slice(half * tm, (half + 1) * tm)
        h = _normed(x_ref, g_ref, tok)
        if perm:
            h = jnp.dot(p_ref[...], h, preferred_element_type=jnp.float32).astype(jnp.bfloat16)
        for which, w_ref in enumerate((wq_ref, wk_ref, wv_ref)):
            acc = jnp.dot(h, w_ref[...], preferred_element_type=jnp.float32)
            for slot in range(N_SLOTS):
                cs = slice(slot * HEAD_DIM, (slot + 1) * HEAD_DIM)
                t = acc[:, cs]
                if which < 2:
                    cos_ref, sin_ref = (cq_ref, sq_ref) if which == 0 else (ck_ref, sk_ref)
                    partner = jnp.where(first_half,
                                        pltpu.roll(t, HEAD_DIM - ROT_DIM // 2, axis=1),
                                        pltpu.roll(t, ROT_DIM // 2, axis=1))
                    t = t * cos_ref[tok, :] + partner * sin_ref[tok, :]
                if perm or dil == 1:
                    for r in range(dil):
                        o_ref[which, r, half * rows:(half + 1) * rows, cs] = (
                            t[r * rows:(r + 1) * rows].astype(o_ref.dtype))
                else:
                    scr[slot] = t
            if not perm and dil > 1:
                for slot in range(N_SLOTS):
                    for r in range(dil):
                        o_ref[which, r, half * rows:(half + 1) * rows, slot * HEAD_DIM:(slot + 1) * HEAD_DIM] = (
                            scr[slot, pl.ds(r, rows, stride=dil), :].astype(o_ref.dtype))


def _class_major(a, tm, dil):
    n_tiles = a.shape[0] // tm
    return a.reshape(n_tiles, tm // dil, dil, -1).transpose(0, 2, 1, 3).reshape(a.shape)


def _class_major_onehot(tm, dil):
    src = np.arange(tm).reshape(tm // dil, dil).T.reshape(tm)
    onehot = np.zeros((tm, tm), np.float32)
    onehot[np.arange(tm), src] = 1.0
    return jnp.asarray(onehot, jnp.bfloat16)


def _proj_qkv(x, gain, wb, tabs, dil, group):
    B, L, D = x.shape
    half = PROJ_TM
    tm = 2 * half
    n = L // dil
    perm = dil >= PERM_MIN_DIL
    tab_spec = pl.BlockSpec((tm, HEAD_DIM), lambda b, i: (i, 0))
    w_specs = [pl.BlockSpec((D, W_BLOCK), functools.partial(lambda b, i, c: (0, c), c=t * N_GROUPS + group),
                            pipeline_mode=pl.Buffered(1)) for t in range(3)]
    in_specs = [pl.BlockSpec((None, tm, D), lambda b, i: (b, i, 0)),
                pl.BlockSpec((1, D), lambda b, i: (0, 0))] + w_specs + [tab_spec] * 4
    args = [x, gain, wb, wb, wb]
    scratch = []
    if perm:
        args += list(tabs) + [_class_major_onehot(half, dil)]
        in_specs.append(pl.BlockSpec((half, half), lambda b, i: (0, 0), pipeline_mode=pl.Buffered(1)))
    else:
        args += list(tabs)
        if dil > 1:
            scratch = [pltpu.VMEM((N_SLOTS, half, HEAD_DIM), jnp.float32)]
    return pl.pallas_call(
        functools.partial(_proj_qkv_kernel, dil=dil, perm=perm),
        grid=(B, L // tm),
        in_specs=in_specs,
        out_specs=pl.BlockSpec((3, None, dil, tm // dil, ATTN_WIDTH), lambda b, i: (0, b, 0, i, 0)),
        out_shape=jax.ShapeDtypeStruct((3, B, dil, n, ATTN_WIDTH), jnp.bfloat16),
        scratch_shapes=scratch,
        compiler_params=_cparams(2),
        name=f"proj_qkv_d{dil}",
    )(*args)


def _rope_tables(L):
    inv_freq = jnp.power(ROPE_THETA, -jnp.arange(0, ROT_DIM, 2, dtype=jnp.float32) / ROT_DIM)
    ang = jnp.arange(L, dtype=jnp.float32)[:, None] * inv_freq[None, :]
    cos, sin = jnp.cos(ang), jnp.sin(ang)
    ones = jnp.ones((L, HEAD_DIM - ROT_DIM), jnp.float32)
    cos_t = jnp.concatenate([cos, cos, ones], axis=1)
    sin_t = jnp.concatenate([-sin, sin, 0.0 * ones], axis=1)
    scale = 1.0 / math.sqrt(HEAD_DIM)
    return cos_t * scale, sin_t * scale, cos_t, sin_t


def _attn_kernel(q_ref, kp_ref, kc_ref, kn_ref, vp_ref, vc_ref, vn_ref, o_ref, lse_ref, kw, vw, *, n):
    tq = q_ref.shape[0]
    i = pl.program_id(1)
    kw[0:RADIUS] = kp_ref[...]
    kw[RADIUS:RADIUS + tq] = kc_ref[...]
    kw[RADIUS + tq:] = kn_ref[...]
    vw[0:RADIUS] = vp_ref[...]
    vw[RADIUS:RADIUS + tq] = vc_ref[...]
    vw[RADIUS + tq:] = vn_ref[...]
    win = ATTN_SUB + 2 * RADIUS
    row = lax.broadcasted_iota(jnp.int32, (ATTN_SUB, win), 0)
    col = lax.broadcasted_iota(jnp.int32, (ATTN_SUB, win), 1)
    band = (col >= row) & (col <= row + 2 * RADIUS)
    lane = lax.broadcasted_iota(jnp.int32, (ATTN_SUB, LANES), 1)
    nsub = tq // ATTN_SUB
    for s in range(nsub):
        q0 = i * tq + s * ATTN_SUB
        valid = band
        if s == 0:
            valid = valid & (col >= RADIUS - q0)
        if s == nsub - 1:
            valid = valid & (col < n - q0 + RADIUS)
        lse_tile = jnp.zeros((ATTN_SUB, LANES), jnp.float32)
        for h in range(N_SLOTS):
            cs = slice(h * HEAD_DIM, (h + 1) * HEAD_DIM)
            qb = q_ref[s * ATTN_SUB:(s + 1) * ATTN_SUB, cs]
            kb = kw[s * ATTN_SUB:s * ATTN_SUB + win, cs]
            vb = vw[s * ATTN_SUB:s * ATTN_SUB + win, cs]
            sc = lax.dot_general(qb, kb, (((1,), (1,)), ((), ())), preferred_element_type=jnp.float32)
            sc = jnp.where(valid, sc, -1e30)
            m = jnp.max(sc, axis=1, keepdims=True)
            p = jnp.exp(sc - m)
            l = jnp.sum(p, axis=1, keepdims=True)
            o = jnp.dot(p.astype(jnp.bfloat16), vb, preferred_element_type=jnp.float32)
            o_ref[s * ATTN_SUB:(s + 1) * ATTN_SUB, cs] = o / l
            lse_tile = jnp.where(lane == h, m + jnp.log(l), lse_tile)
        lse_ref[s * ATTN_SUB:(s + 1) * ATTN_SUB, :] = lse_tile


def _attention(qkv):
    _, S, n, W = qkv.shape
    tq = min(ATTN_TQ, n)
    r = tq // RADIUS
    last = n // RADIUS - 1

    def cur(which):
        return pl.BlockSpec((None, None, tq, W), lambda s, i: (which, s, i, 0))

    def prev(which):
        return pl.BlockSpec((None, None, RADIUS, W), lambda s, i: (which, s, jnp.maximum(i * r - 1, 0), 0))

    def nxt(which):
        return pl.BlockSpec((None, None, RADIUS, W), lambda s, i: (which, s, jnp.minimum((i + 1) * r, last), 0))

    return pl.pallas_call(
        functools.partial(_attn_kernel, n=n),
        grid=(S, n // tq),
        in_specs=[cur(0), prev(1), cur(1), nxt(1), prev(2), cur(2), nxt(2)],
        out_specs=[pl.BlockSpec((None, tq, W), lambda s, i: (s, i, 0)),
                   pl.BlockSpec((None, tq, LANES), lambda s, i: (s, i, 0))],
        out_shape=[jax.ShapeDtypeStruct((S, n, W), jnp.float32),
                   jax.ShapeDtypeStruct((S, n, LANES), jnp.float32)],
        scratch_shapes=[pltpu.VMEM((tq + 2 * RADIUS, W), jnp.bfloat16),
                        pltpu.VMEM((tq + 2 * RADIUS, W), jnp.bfloat16)],
        compiler_params=_cparams(2),
        name="attn",
    )(qkv, qkv, qkv, qkv, qkv, qkv, qkv)


def _shift_rows(u, edge_row, down):
    tl = u.shape[0]
    rolled = pltpu.roll(u, 1 if down else tl - 1, axis=0)
    r8 = lax.broadcasted_iota(jnp.int32, (SUBLANES, u.shape[1]), 0)
    if down:
        head = jnp.where(r8 == 0, edge_row, rolled[:SUBLANES])
        return jnp.concatenate([head, rolled[SUBLANES:]], axis=0)
    tail = jnp.where(r8 == SUBLANES - 1, edge_row, rolled[tl - SUBLANES:])
    return jnp.concatenate([rolled[:tl - SUBLANES], tail], axis=0)


def _hyena_pre_kernel(u_ref, up_ref, un_ref, z_ref, cw_ref, cb_ref, p_ref, g_ref, m_ref, *, n2, nblk):
    tl = u_ref.shape[0]
    i = pl.program_id(1)
    has_prev = (i > 0).astype(jnp.float32)
    has_next = (i < nblk - 1).astype(jnp.float32)
    C = HYENA_WIDTH
    rows = tl // n2
    perm = p_ref[...]

    def conv(cs):
        u = u_ref[:, cs].astype(jnp.float32)
        edge_prev = up_ref[HALO_ROWS - 1:HALO_ROWS, cs].astype(jnp.float32) * has_prev
        edge_next = un_ref[0:1, cs].astype(jnp.float32) * has_next
        u_prev = _shift_rows(u, edge_prev, True)
        u_next = _shift_rows(u, edge_next, False)
        return u_prev * cw_ref[0:1, cs] + u * cw_ref[1:2, cs] + u_next * cw_ref[2:3, cs] + cb_ref[:, cs]

    for c in range(C // LANES):
        ls = slice(c * LANES, (c + 1) * LANES)
        x0, x1, vh = (conv(slice(k * C + c * LANES, k * C + (c + 1) * LANES)) for k in range(3))
        z = z_ref[:, ls].astype(jnp.float32)
        m_hi, m_lo = _split_bf16(x0 * (z * _sigmoid(z)))
        stack = jnp.concatenate([(vh * x1).astype(jnp.bfloat16), m_hi, m_lo], axis=1)
        res = jnp.dot(perm, stack, preferred_element_type=jnp.float32)
        g = res[:, :LANES]
        m = res[:, LANES:2 * LANES] + res[:, 2 * LANES:]
        for r in range(n2):
            g_ref[r, :, ls] = g[r * rows:(r + 1) * rows].astype(g_ref.dtype)
            m_ref[r, :, ls] = m[r * rows:(r + 1) * rows]


def _hyena_pre(u_hy, gates, conv_w, conv_b, n2):
    B, L, C3 = u_hy.shape
    C = HYENA_WIDTH
    tl = HY_TL
    nblk = L // tl
    per = tl // HALO_ROWS
    last_halo = L // HALO_ROWS - 1
    out_spec = pl.BlockSpec((None, n2, tl // n2, C), lambda b, i: (b, 0, i, 0))
    out_sds = jax.ShapeDtypeStruct((B, n2, L // n2, C), jnp.float32)
    g_sds = jax.ShapeDtypeStruct(out_sds.shape, jnp.bfloat16 if (tl // n2) % BF16_TILE_ROWS == 0 else jnp.float32)
    return pl.pallas_call(
        functools.partial(_hyena_pre_kernel, n2=n2, nblk=nblk),
        grid=(B, nblk),
        in_specs=[pl.BlockSpec((None, tl, C3), lambda b, i: (b, i, 0)),
                  pl.BlockSpec((None, HALO_ROWS, C3), lambda b, i: (b, jnp.maximum(i * per - 1, 0), 0)),
                  pl.BlockSpec((None, HALO_ROWS, C3), lambda b, i: (b, jnp.minimum((i + 1) * per, last_halo), 0)),
                  pl.BlockSpec((None, tl, C), lambda b, i: (b, i, 1)),
                  pl.BlockSpec((3, C3), lambda b, i: (0, 0)),
                  pl.BlockSpec((1, C3), lambda b, i: (0, 0)),
                  pl.BlockSpec((tl, tl), lambda b, i: (0, 0), pipeline_mode=pl.Buffered(1))],
        out_specs=[out_spec, out_spec],
        out_shape=[g_sds, out_sds],
        compiler_params=_cparams(2),
        name="hyena_pre",
    )(u_hy, u_hy, u_hy, gates, conv_w, conv_b, _class_major_onehot(tl, n2))


POS_VALID, POS_ZERO = FILTER_EMB, FILTER_EMB + 1


def _dot3(a, wh_ref, wl_ref):
    ah, al = _split_bf16(a)
    wh = wh_ref[...]
    return (jnp.dot(ah, wh, preferred_element_type=jnp.float32)
            + jnp.dot(al, wh, preferred_element_type=jnp.float32)
            + jnp.dot(ah, wl_ref[...], preferred_element_type=jnp.float32))


def _filter_kernel(p_ref, w1h, w1l, b1_ref, w2h, w2l, b2_ref, w3h, w3l, b3_ref, w4fh, w4fl, w4bh, w4bl,
                   fr_ref, dl_ref, sk_ref, o_ref):
    p = p_ref[...]
    fr = fr_ref[...]
    p2 = jnp.concatenate([p[:DFT_HALF], p[DFT_HALF:]], axis=1)
    hdn = jnp.sin(fr * (_dot3(p2, w1h, w1l) + b1_ref[...]))
    hdn = jnp.sin(fr * (_dot3(hdn, w2h, w2l) + b2_ref[...]))
    hdn = jnp.sin(fr * (_dot3(hdn, w3h, w3l) + b3_ref[...]))
    scale = jnp.exp(-p[:, 0:1] * dl_ref[...]) * p[:, POS_VALID:POS_VALID + 1]
    skip = p[:, POS_ZERO:POS_ZERO + 1] * sk_ref[...]
    for d, (wh, wl) in enumerate(((w4fh, w4fl), (w4bh, w4bl))):
        rs = slice(d * DFT_HALF, (d + 1) * DFT_HALF)
        o_ref[rs, :] = (_dot3(hdn, wh, wl) * scale[rs] + skip[rs]).astype(o_ref.dtype)


def _filter_lag_table(L, n2):
    bands = (FILTER_EMB - 1) // 2
    f32 = np.float32
    n = np.arange(2 * L).reshape(2 * L // n2, n2).T.reshape(2 * L)
    lag = np.where(n < L, n, 2 * L - n)
    idx = np.where(n == L, 0, lag).astype(f32)[:, None]
    t = np.where(idx == L - 1, f32(1.0), idx * f32(1.0 / (L - 1))).astype(f32)
    w = (f32(2.0 * math.pi) * idx / f32(L)).astype(f32)
    f = np.linspace(1e-4, bands - 1, bands, dtype=f32)[None, :]
    arg = (f * w).astype(f32)
    flags = np.stack([n != L, n == 0], axis=1).astype(f32)
    tab = np.concatenate([t, np.cos(arg), -np.sin(arg), flags], axis=1).astype(f32)
    return jnp.pad(jnp.asarray(tab), ((0, 0), (0, LANES - tab.shape[1])))


def _filter(L, n2, lp):
    C = HYENA_WIDTH
    tab = _filter_lag_table(L, n2)
    H = FILTER_HIDDEN
    assert 2 * H == LANES
    zeros = lambda r, c: jnp.zeros((r, c), jnp.float32)
    blockdiag = lambda w: jnp.block([[w, zeros(*w.shape)], [zeros(*w.shape), w]])
    w1p = jnp.concatenate([lp["filt_w1"], zeros(LANES - FILTER_EMB, H)], axis=0)
    w4 = lp["filt_w4"]
    w4f = jnp.concatenate([w4[:, :C], zeros(H, C)], axis=0)
    w4b = jnp.concatenate([zeros(H, C), w4[:, C:]], axis=0)
    twice = lambda v: jnp.concatenate([v, v])[None, :]
    min_decay = math.log(DECAY_TARGET) / DECAY_FAST_PCT
    max_decay = math.log(DECAY_TARGET) / DECAY_SLOW_PCT
    deltas = jnp.abs(jnp.linspace(min_decay, max_decay, C, dtype=jnp.float32))[None, :]
    full = lambda a: pl.BlockSpec(a.shape, lambda i: (0,) * a.ndim)
    args = [*_split_bf16(blockdiag(w1p)), twice(lp["filt_b1"]), *_split_bf16(blockdiag(lp["filt_w2"])),
            twice(lp["filt_b2"]), *_split_bf16(blockdiag(lp["filt_w3"])), twice(lp["filt_b3"]),
            *_split_bf16(w4f), *_split_bf16(w4b), twice(lp["filt_freq"]), deltas, lp["hyena_skip"][None, :]]
    k = pl.pallas_call(
        _filter_kernel,
        grid=(n2,),
        in_specs=[pl.BlockSpec((DFT_N1, LANES), lambda i: (i, 0))] + [full(a) for a in args],
        out_specs=pl.BlockSpec((DFT_N1, C), lambda i: (i, 0)),
        out_shape=jax.ShapeDtypeStruct((2 * L, C), jnp.bfloat16),
        compiler_params=_cparams(1),
        name="filter",
    )(tab, *args)
    return k.reshape(n2, DFT_N1, C)


def _dft_tables(n2):
    N = DFT_N1 * n2
    k1 = np.arange(DFT_ROWS)
    live = (k1 <= DFT_HALF).astype(np.float64)[:, None]
    a = 2.0 * np.pi * ((k1[:, None] * np.arange(DFT_N1)[None, :]) % DFT_N1) / DFT_N1
    b = 2.0 * np.pi * (k1[:, None] * np.arange(n2)[None, :]) / N
    fwd = np.concatenate([np.cos(a) * live, -np.sin(a) * live], axis=0)
    weight = (np.where((k1 == 0) | (k1 == DFT_HALF), 1.0, 2.0) / N)[:, None]
    inv = np.concatenate([np.cos(a) * live * weight, -np.sin(a) * live * weight], axis=0).T[:DFT_HALF]
    tw = np.stack([np.cos(b), np.sin(b)], axis=2).reshape(DFT_ROWS // SUBLANES, SUBLANES, 2 * n2)
    return jnp.asarray(fwd, jnp.bfloat16), jnp.asarray(inv, jnp.bfloat16), jnp.asarray(tw, jnp.float32)


def _twiddle_factors(tw_c, n2):
    bc = lambda col: jnp.broadcast_to(tw_c[:, col:col + 1], (SUBLANES, LANES))
    return [None if r == 0 else (bc(2 * r), bc(2 * r + 1)) for r in range(n2)]


def _twiddle(z, factors, sign):
    if factors is None:
        return z
    re, im = z
    cb, sb = factors
    if sign < 0:
        return re * cb + im * sb, im * cb - re * sb
    return re * cb - im * sb, re * sb + im * cb


def _filter_spectrum_kernel(k_ref, f_ref, tw_ref, o_ref, *, n2, slabs):
    nj = n2 // slabs
    j = pl.program_id(1)
    for s in range(slabs):
        o_ref[j * slabs + s] = jnp.dot(f_ref[...], k_ref[s], preferred_element_type=jnp.float32)

    @pl.when(j == nj - 1)
    def _across_slabs():
        def body(c, carry):
            re = pl.ds(pl.multiple_of(c * SUBLANES, SUBLANES), SUBLANES)
            im = pl.ds(pl.multiple_of(c * SUBLANES + DFT_ROWS, SUBLANES), SUBLANES)
            tw = _twiddle_factors(tw_ref[c], n2)
            for t in range(o_ref.shape[2] // LANES):
                ls = slice(t * LANES, (t + 1) * LANES)
                ys = _fft_list([_twiddle((o_ref[r, re, ls], o_ref[r, im, ls]), tw[r], -1)
                                for r in range(n2)], -1)
                for k2 in range(n2):
                    o_ref[k2, re, ls] = ys[k2][0]
                    o_ref[k2, im, ls] = ys[k2][1]
            return carry

        lax.fori_loop(0, DFT_ROWS // SUBLANES, body, 0, unroll=11)


def _const_spec(a):
    return pl.BlockSpec(a.shape, lambda *_: (0,) * a.ndim, pipeline_mode=pl.Buffered(1))


def _filter_spectrum(k, fwd, tw, slabs=8):
    n2, K, C = k.shape
    slabs = min(slabs, n2)
    tc = 2 * LANES
    return pl.pallas_call(
        functools.partial(_filter_spectrum_kernel, n2=n2, slabs=slabs),
        grid=(C // tc, n2 // slabs),
        in_specs=[pl.BlockSpec((slabs, K, tc), lambda c, j: (j, 0, c)), _const_spec(fwd), _const_spec(tw)],
        out_specs=pl.BlockSpec((n2, DFT_M, tc), lambda c, j: (0, 0, c), pipeline_mode=pl.Buffered(1)),
        out_shape=jax.ShapeDtypeStruct((n2, DFT_M, C), jnp.float32),
        compiler_params=pltpu.CompilerParams(dimension_semantics=("parallel", "arbitrary"),
                                             vmem_limit_bytes=VMEM_LIMIT),
        name="filter_spectrum",
    )(k, fwd, tw)


def _fft_list(xs, sign):
    n = len(xs)
    if n == 1:
        return xs
    even = _fft_list(xs[0::2], sign)
    odd = _fft_list(xs[1::2], sign)
    out = [None] * n
    for k in range(n // 2):
        er, ei = even[k]
        pr, pi = odd[k]
        if k == 0:
            tr, ti = pr, pi
        elif 4 * k == n:
            tr, ti = (pi, -pr) if sign < 0 else (-pi, pr)
        else:
            ang = sign * 2.0 * math.pi * k / n
            wr, wi = math.cos(ang), math.sin(ang)
            tr, ti = pr * wr - pi * wi, pr * wi + pi * wr
        out[k] = (er + tr, ei + ti)
        out[k + n // 2] = (er - tr, ei - ti)
    return out


def _long_conv_kernel(x_ref, f_ref, g_ref, tw_ref, gate_ref, kf_ref, o_ref, spec, *, n2, slabs, resident):
    nj = n2 // slabs
    j = pl.program_id(2)
    tc = spec.shape[2]
    fwd_base = j * slabs if resident else 0
    inv_base = (j - nj) * slabs if resident else 0

    @pl.when(j < nj)
    def _forward():
        for s in range(slabs):
            spec[j * slabs + s] = jnp.dot(f_ref[...], x_ref[fwd_base + s].astype(jnp.bfloat16),
                                          preferred_element_type=jnp.float32)

    @pl.when(j == nj - 1)
    def _across_slabs():
        def body(c, carry):
            re = pl.ds(pl.multiple_of(c * SUBLANES, SUBLANES), SUBLANES)
            im = pl.ds(pl.multiple_of(c * SUBLANES + DFT_ROWS, SUBLANES), SUBLANES)
            tw = _twiddle_factors(tw_ref[c], n2)
            for t in range(tc // LANES):
                ls = slice(t * LANES, (t + 1) * LANES)
                ys = _fft_list([_twiddle((spec[r, re, ls], spec[r, im, ls]), tw[r], -1)
                                for r in range(n2)], -1)
                prod = []
                for k2 in range(n2):
                    yr, yi = ys[k2]
                    fr, fi = kf_ref[k2, re, ls], kf_ref[k2, im, ls]
                    prod.append((yr * fr - yi * fi, yr * fi + yi * fr))
                vs = _fft_list(prod, +1)
                for r in range(n2):
                    vr, vi = _twiddle(vs[r], tw[r], +1)
                    spec[r, re, ls] = vr
                    spec[r, im, ls] = vi
            return carry

        lax.fori_loop(0, DFT_ROWS // SUBLANES, body, 0, unroll=11)

    @pl.when(j >= nj)
    def _inverse():
        for s in range(slabs):
            v = spec[(j - nj) * slabs + s].astype(jnp.bfloat16)
            o_ref[inv_base + s] = (jnp.dot(g_ref[...], v, preferred_element_type=jnp.float32)
                                   * gate_ref[inv_base + s])


def _long_conv(x, gate, fwd, inv, tw, kf, slabs=8):
    B, n2, K, C = x.shape
    slabs = min(slabs, n2)
    nj = n2 // slabs
    spectra = lambda tc: 2 * n2 * DFT_M * tc * 4
    tc = 2 * LANES if spectra(2 * LANES) <= LONG_CONV_VMEM else LANES
    whole = 2 * n2 * K * tc * (4 + 4 + x.dtype.itemsize)
    resident = spectra(tc) + whole <= LONG_CONV_RESIDENT_VMEM
    fwd_j = lambda j: jnp.minimum(j, nj - 1)
    inv_j = lambda j: jnp.maximum(j - nj, 0)
    if resident:
        io_spec = pl.BlockSpec((None, n2, K, tc), lambda c, b, j: (b, 0, 0, c))
        x_spec = io_spec
    else:
        io_spec = pl.BlockSpec((None, slabs, K, tc), lambda c, b, j: (b, inv_j(j), 0, c))
        x_spec = pl.BlockSpec((None, slabs, K, tc), lambda c, b, j: (b, fwd_j(j), 0, c))
    f_spec = pl.BlockSpec((DFT_M, K), lambda c, b, j: (0, 0), pipeline_mode=pl.Buffered(1))
    return pl.pallas_call(
        functools.partial(_long_conv_kernel, n2=n2, slabs=slabs, resident=resident),
        grid=(C // tc, B, 2 * nj),
        in_specs=[x_spec, f_spec, _const_spec(inv), _const_spec(tw), io_spec,
                  pl.BlockSpec((n2, DFT_M, tc), lambda c, b, j: (0, 0, c), pipeline_mode=pl.Buffered(1))],
        out_specs=io_spec,
        out_shape=jax.ShapeDtypeStruct((B, n2, K, C), jnp.float32),
        scratch_shapes=[pltpu.VMEM((n2, DFT_M, tc), jnp.float32)],
        compiler_params=pltpu.CompilerParams(dimension_semantics=("parallel", "parallel", "arbitrary"),
                                             vmem_limit_bytes=VMEM_LIMIT),
        name="long_conv",
    )(x, fwd, inv, tw, gate, kf)


def _regroup_to_tokens(blk_ref, scr, dil, width):
    m = blk_ref.shape[1]
    for c in range(width // LANES):
        for r in range(dil):
            scr[c, pl.ds(r, m, stride=dil), :] = blk_ref[r, :, c * LANES:(c + 1) * LANES]


def _merge_kernel(o0_ref, l0_ref, o1_ref, l1_ref, o2_ref, l2_ref, hy_ref, za_ref, ga_ref, gh_ref,
                  wa_ref, wh_ref, out_ref, so1, so2, sl1, sl2, shy, attn_scr, hy_scr, *, n2):
    d1, d2 = ATTN_GROUPS[1][1], ATTN_GROUPS[2][1]
    _regroup_to_tokens(o1_ref, so1, d1, ATTN_WIDTH)
    _regroup_to_tokens(o2_ref, so2, d2, ATTN_WIDTH)
    _regroup_to_tokens(l1_ref, sl1, d1, LANES)
    _regroup_to_tokens(l2_ref, sl2, d2, LANES)
    _regroup_to_tokens(hy_ref, shy, n2, HYENA_WIDTH)
    tm = out_ref.shape[0]
    for half in range(2):
        rs = slice(half * tm // 2, (half + 1) * tm // 2)
        l0, l1, l2 = l0_ref[rs, :], sl1[0, rs, :], sl2[0, rs, :]
        mx = jnp.maximum(jnp.maximum(l0, l1), l2)
        e0, e1, e2 = jnp.exp(l0 - mx), jnp.exp(l1 - mx), jnp.exp(l2 - mx)
        inv = 1.0 / (e0 + e1 + e2)
        a0, a1, a2 = e0 * inv, e1 * inv, e2 * inv
        for h in range(N_SLOTS):
            cs = slice(h * HEAD_DIM, (h + 1) * HEAD_DIM)
            mix = (a0[:, h:h + 1] * o0_ref[rs, cs] + a1[:, h:h + 1] * so1[h, rs, :]
                   + a2[:, h:h + 1] * so2[h, rs, :])
            z = za_ref[rs, cs].astype(jnp.float32)
            attn_scr[rs, cs] = (mix * (z * _sigmoid(z))).astype(jnp.bfloat16)
            hy_scr[rs, cs] = shy[h, rs, :].astype(jnp.bfloat16)
        br_a = jnp.dot(attn_scr[rs, :], wa_ref[...], preferred_element_type=jnp.float32)
        br_h = jnp.dot(hy_scr[rs, :], wh_ref[...], preferred_element_type=jnp.float32)
        merged = (_sigmoid(ga_ref[rs, :].astype(jnp.float32)) * br_a
                  + _sigmoid(gh_ref[rs, :].astype(jnp.float32)) * br_h)
        out_ref[rs, :] = merged.astype(out_ref.dtype)


def _merge(outs, lses, hyena, gates, w_ba, w_bh, n2):
    B, d0, L, W = outs[0].shape
    D = w_ba.shape[1]
    tm = MERGE_TM
    d1, d2 = ATTN_GROUPS[1][1], ATTN_GROUPS[2][1]

    def cls(d, width):
        return pl.BlockSpec((None, d, tm // d, width), lambda b, i: (b, 0, i, 0))

    in_specs = [pl.BlockSpec((None, None, tm, W), lambda b, i: (b, 0, i, 0)),
                pl.BlockSpec((None, None, tm, LANES), lambda b, i: (b, 0, i, 0)),
                cls(d1, W), cls(d1, LANES), cls(d2, W), cls(d2, LANES), cls(n2, HYENA_WIDTH),
                pl.BlockSpec((None, tm, W), lambda b, i: (b, i, 0)),
                pl.BlockSpec((None, tm, D), lambda b, i: (b, i, 1)),
                pl.BlockSpec((None, tm, D), lambda b, i: (b, i, 2)),
                pl.BlockSpec(w_ba.shape, lambda b, i: (0, 0), pipeline_mode=pl.Buffered(1)),
                pl.BlockSpec(w_bh.shape, lambda b, i: (0, 0), pipeline_mode=pl.Buffered(1))]
    wt = W // LANES
    return pl.pallas_call(
        functools.partial(_merge_kernel, n2=n2),
        grid=(B, L // tm),
        in_specs=in_specs,
        out_specs=pl.BlockSpec((None, tm, D), lambda b, i: (b, i, 0)),
        out_shape=jax.ShapeDtypeStruct((B, L, D), jnp.bfloat16),
        scratch_shapes=[pltpu.VMEM((wt, tm, LANES), jnp.float32), pltpu.VMEM((wt, tm, LANES), jnp.float32),
                        pltpu.VMEM((1, tm, LANES), jnp.float32), pltpu.VMEM((1, tm, LANES), jnp.float32),
                        pltpu.VMEM((wt, tm, LANES), jnp.float32),
                        pltpu.VMEM((tm, W), jnp.bfloat16), pltpu.VMEM((tm, HYENA_WIDTH), jnp.bfloat16)],
        compiler_params=_cparams(2),
        name="merge",
    )(outs[0], lses[0], outs[1], lses[1], outs[2], lses[2], hyena, gates, gates, gates, w_ba, w_bh)


def _out_kernel(m_ref, w_ref, x_ref, g_ref, o_ref, *, final_norm):
    y = x_ref[...] + jnp.dot(m_ref[...], w_ref[...], preferred_element_type=jnp.float32)
    if final_norm:
        ms = jnp.mean(y * y, axis=-1, keepdims=True)
        y = y * lax.rsqrt(ms + RMS_EPS) * g_ref[...]
    o_ref[...] = y


def _out_proj(merged, w_out, x, final_gain, final_norm):
    B, L, D = x.shape
    tm = MERGE_TM
    tok = pl.BlockSpec((None, tm, D), lambda b, i: (b, i, 0))
    return pl.pallas_call(
        functools.partial(_out_kernel, final_norm=final_norm),
        grid=(B, L // tm),
        in_specs=[tok, pl.BlockSpec(w_out.shape, lambda b, i: (0, 0), pipeline_mode=pl.Buffered(1)),
                  tok, pl.BlockSpec((1, D), lambda b, i: (0, 0))],
        out_specs=tok,
        out_shape=jax.ShapeDtypeStruct((B, L, D), jnp.float32),
        compiler_params=_cparams(2),
        name="out_proj",
    )(merged, w_out, x, final_gain)


def _prep_layer(i, norm_gain, w_in, conv_w, conv_b, filt_w1, filt_b1, filt_w2, filt_b2, filt_w3, filt_b3,
                filt_w4, filt_freq, hyena_skip, w_branch_attn, w_branch_hyena, w_out):
    bf = jnp.bfloat16
    return dict(
        gain=norm_gain[i][None, :],
        w_in=w_in[i].astype(bf),
        conv_w=conv_w[i], conv_b=conv_b[i][None, :],
        filt_w1=filt_w1[i], filt_b1=filt_b1[i], filt_w2=filt_w2[i], filt_b2=filt_b2[i],
        filt_w3=filt_w3[i], filt_b3=filt_b3[i], filt_w4=filt_w4[i], filt_freq=filt_freq[i],
        hyena_skip=hyena_skip[i],
        w_ba=w_branch_attn[i].astype(bf), w_bh=w_branch_hyena[i].astype(bf), w_out=w_out[i].astype(bf))


def _rope_tables_by_dilation(L):
    base = _rope_tables(L)
    return {dil: tuple(_class_major(t, PROJ_TM, dil) for t in base) if dil >= PERM_MIN_DIL else base
            for _, dil in ATTN_GROUPS}


def _layer(x, lp, final_gain, final_norm, rope):
    B, L, D = x.shape
    assert L % (2 * DFT_HALF) == 0 and L % (2 * PROJ_TM) == 0 and D == 2 * W_BLOCK
    n2 = L // DFT_HALF

    outs, lses = [], []
    for g, (_, dil) in enumerate(ATTN_GROUPS):
        qkv = _proj_qkv(x, lp["gain"], lp["w_in"], rope[dil], dil, g)
        n = L // dil
        o, lse = _attention(qkv.reshape(3, B * dil, n, ATTN_WIDTH))
        outs.append(o.reshape(B, dil, n, ATTN_WIDTH))
        lses.append(lse.reshape(B, dil, n, LANES))

    gate_cols = (lambda s: jnp.where(s == 0, GATE_BLOCKS[0], GATE_BLOCKS[2] + 2 * (s - 1)),
                 lambda s: jnp.where(s == 0, GATE_BLOCKS[1], GATE_BLOCKS[2] + 2 * (s - 1) + 1))
    gates = _proj_plain(x, lp["gain"], lp["w_in"], gate_cols, 3, jnp.bfloat16)
    uhy_cols = tuple(functools.partial(lambda s, c: c, c=UHY_BLOCK + k) for k in range(3))
    u_hy = _proj_plain(x, lp["gain"], lp["w_in"], uhy_cols, 1, jnp.bfloat16)
    gated, mult = _hyena_pre(u_hy, gates, lp["conv_w"], lp["conv_b"], n2)
    fwd_mat, inv_mat, tw = _dft_tables(n2)
    kf = _filter_spectrum(_filter(L, n2, lp), fwd_mat, tw)
    hyena = _long_conv(gated, mult, fwd_mat, inv_mat, tw, kf)

    merged = _merge(outs, lses, hyena, gates, lp["w_ba"], lp["w_bh"], n2)
    return _out_proj(merged, lp["w_out"], x, final_gain, final_norm)


def kernel(x_prompt, x_sample, norm_gain, w_in, conv_w, conv_b, filt_w1, filt_b1, filt_w2, filt_b2, filt_w3, filt_b3, filt_w4, filt_freq, hyena_skip, w_branch_attn, w_branch_hyena, w_out, final_gain):
    depth = w_in.shape[0]
    layers = [_prep_layer(i, norm_gain, w_in, conv_w, conv_b, filt_w1, filt_b1, filt_w2, filt_b2, filt_w3,
                          filt_b3, filt_w4, filt_freq, hyena_skip, w_branch_attn, w_branch_hyena, w_out)
              for i in range(depth)]
    fg = final_gain[None, :]
    rope = _rope_tables_by_dilation(max(x_prompt.shape[1], x_sample.shape[1]))

    def trunk(x):
        for i, lp in enumerate(layers):
            x = _layer(x, lp, fg, final_norm=(i == depth - 1), rope=rope)
        return x

    return (trunk(x_prompt), trunk(x_sample))
```

```python
import functools
import math

import jax
import jax.numpy as jnp
import numpy as np
from jax import lax
from jax.experimental import pallas as pl
from jax.experimental.pallas import tpu as pltpu

HEAD_DIM = 128
N_SLOTS = 8
ATTN_GROUPS = ((128, 1), (512, 4), (2048, 16))
N_GROUPS = len(ATTN_GROUPS)
ATTN_WIDTH = N_SLOTS * HEAD_DIM
QKV_WIDTH = N_GROUPS * ATTN_WIDTH
ROT_DIM = HEAD_DIM // 4
ROPE_THETA = 500000.0
HYENA_WIDTH = 1024
FILTER_EMB = 33
FILTER_HIDDEN = 64
DECAY_TARGET = 1e-2
DECAY_FAST_PCT = 0.3
DECAY_SLOW_PCT = 1.5
RMS_EPS = 1e-6
RADIUS = 64

LANES = 128
SUBLANES = 8
VMEM_LIMIT = 56 * 1024 * 1024
LONG_CONV_VMEM = 36 * 1024 * 1024
LONG_CONV_RESIDENT_VMEM = 48 * 1024 * 1024

DFT_N1 = 512
DFT_HALF = DFT_N1 // 2
DFT_ROWS = 264
DFT_M = 2 * DFT_ROWS

PROJ_TM = 512
W_BLOCK = 1024
_ZA_BLOCK = 3 * QKV_WIDTH // W_BLOCK
UHY_BLOCK = _ZA_BLOCK + ATTN_WIDTH // W_BLOCK
_ZH_BLOCK = UHY_BLOCK + 3 * HYENA_WIDTH // W_BLOCK
GATE_BLOCKS = (_ZA_BLOCK, _ZH_BLOCK, _ZH_BLOCK + HYENA_WIDTH // W_BLOCK)
PERM_MIN_DIL = 8
ATTN_TQ = 512
ATTN_SUB = 128
HY_TL = 512
BF16_TILE_ROWS = 2 * SUBLANES
HALO_ROWS = BF16_TILE_ROWS
MERGE_TM = 512


def _cparams(n_axes):
    return pltpu.CompilerParams(dimension_semantics=("parallel",) * n_axes,
                                vmem_limit_bytes=VMEM_LIMIT)


def _sigmoid(x):
    return 0.5 * jnp.tanh(0.5 * x) + 0.5


def _split_bf16(x):
    hi = x.astype(jnp.bfloat16)
    lo = (x - hi.astype(jnp.float32)).astype(jnp.bfloat16)
    return hi, lo


def _normed(x_ref, g_ref, rows=slice(None)):
    x = x_ref[rows, :]
    ms = jnp.mean(x * x, axis=-1, keepdims=True)
    return (x * lax.rsqrt(ms + RMS_EPS) * g_ref[...]).astype(jnp.bfloat16)


def _proj_plain_kernel(x_ref, g_ref, *rest):
    w_refs, o_ref = rest[:-1], rest[-1]
    tm = x_ref.shape[0]
    for half in range(2):
        rows = slice(half * tm // 2, (half + 1) * tm // 2)
        h = _normed(x_ref, g_ref, rows)
        for k, w_ref in enumerate(w_refs):
            cw = w_ref.shape[1]
            o_ref[rows, k * cw:(k + 1) * cw] = jnp.dot(
                h, w_ref[...], preferred_element_type=jnp.float32).astype(o_ref.dtype)


def _proj_plain(x, gain, wb, col_maps, n_slabs, out_dtype):
    B, L, D = x.shape
    tm = 2 * PROJ_TM
    cs = len(col_maps) * W_BLOCK
    w_specs = [pl.BlockSpec((D, W_BLOCK), functools.partial(lambda s, b, i, f: (0, f(s)), f=f),
                            pipeline_mode=pl.Buffered(1)) for f in col_maps]
    return pl.pallas_call(
        _proj_plain_kernel,
        grid=(n_slabs, B, L // tm),
        in_specs=[pl.BlockSpec((None, tm, D), lambda s, b, i: (b, i, 0)),
                  pl.BlockSpec((1, D), lambda s, b, i: (0, 0))] + w_specs,
        out_specs=pl.BlockSpec((None, tm, cs), lambda s, b, i: (b, i, s)),
        out_shape=jax.ShapeDtypeStruct((B, L, n_slabs * cs), out_dtype),
        compiler_params=_cparams(3),
        name="proj_plain",
    )(x, gain, *([wb] * len(col_maps)))


def _proj_qkv_kernel(x_ref, g_ref, wq_ref, wk_ref, wv_ref, cq_ref, sq_ref, ck_ref, sk_ref, *rest, dil, perm):
    tm = PROJ_TM
    rows = tm // dil
    if perm:
        p_ref, o_ref = rest
    elif dil > 1:
        o_ref, scr = rest
    else:
        (o_ref,) = rest
    lane = lax.broadcasted_iota(jnp.int32, (tm, HEAD_DIM), 1)
    first_half = lane < (ROT_DIM // 2)
    for half in range(x_ref.shape[0] // tm):
        tok = slice(half * tm, (half + 1) * tm)
        h = _normed(x_ref, g_ref, tok)
        if perm:
            h = jnp.dot(p_ref[...], h, preferred_element_type=jnp.float32).astype(jnp.bfloat16)
        for which, w_ref in enumerate((wq_ref, wk_ref, wv_ref)):
            acc = jnp.dot(h, w_ref[...], preferred_element_type=jnp.float32)
            for slot in range(N_SLOTS):
                cs = slice(slot * HEAD_DIM, (slot + 1) * HEAD_DIM)
                t = acc[:, cs]
                if which < 2:
                    cos_ref, sin_ref = (cq_ref, sq_ref) if which == 0 else (ck_ref, sk_ref)
                    partner = jnp.where(first_half,
                                        pltpu.roll(t, HEAD_DIM - ROT_DIM // 2, axis=1),
                                        pltpu.roll(t, ROT_DIM // 2, axis=1))
                    t = t * cos_ref[tok, :] + partner * sin_ref[tok, :]
                if perm or dil == 1:
                    for r in range(dil):
                        o_ref[which, r, half * rows:(half + 1) * rows, cs] = (
                            t[r * rows:(r + 1) * rows].astype(o_ref.dtype))
                else:
                    scr[slot] = t
            if not perm and dil > 1:
                for slot in range(N_SLOTS):
                    for r in range(dil):
                        o_ref[which, r, half * rows:(half + 1) * rows, slot * HEAD_DIM:(slot + 1) * HEAD_DIM] = (
                            scr[slot, pl.ds(r, rows, stride=dil), :].astype(o_ref.dtype))


def _class_major(a, tm, dil):
    n_tiles = a.shape[0] // tm
    return a.reshape(n_tiles, tm // dil, dil, -1).transpose(0, 2, 1, 3).reshape(a.shape)


def _class_major_onehot(tm, dil):
    src = np.arange(tm).reshape(tm // dil, dil).T.reshape(tm)
    onehot = np.zeros((tm, tm), np.float32)
    onehot[np.arange(tm), src] = 1.0
    return jnp.asarray(onehot, jnp.bfloat16)


def _proj_qkv(x, gain, wb, tabs, dil, group):
    B, L, D = x.shape
    half = PROJ_TM
    tm = 2 * half
    n = L // dil
    perm = dil >= PERM_MIN_DIL
    tab_spec = pl.BlockSpec((tm, HEAD_DIM), lambda b, i: (i, 0))
    w_specs = [pl.BlockSpec((D, W_BLOCK), functools.partial(lambda b, i, c: (0, c), c=t * N_GROUPS + group),
                            pipeline_mode=pl.Buffered(1)) for t in range(3)]
    in_specs = [pl.BlockSpec((None, tm, D), lambda b, i: (b, i, 0)),
                pl.BlockSpec((1, D), lambda b, i: (0, 0))] + w_specs + [tab_spec] * 4
    args = [x, gain, wb, wb, wb]
    scratch = []
    if perm:
        args += list(tabs) + [_class_major_onehot(half, dil)]
        in_specs.append(pl.BlockSpec((half, half), lambda b, i: (0, 0), pipeline_mode=pl.Buffered(1)))
    else:
        args += list(tabs)
        if dil > 1:
            scratch = [pltpu.VMEM((N_SLOTS, half, HEAD_DIM), jnp.float32)]
    return pl.pallas_call(
        functools.partial(_proj_qkv_kernel, dil=dil, perm=perm),
        grid=(B, L // tm),
        in_specs=in_specs,
        out_specs=pl.BlockSpec((3, None, dil, tm // dil, ATTN_WIDTH), lambda b, i: (0, b, 0, i, 0)),
        out_shape=jax.ShapeDtypeStruct((3, B, dil, n, ATTN_WIDTH), jnp.bfloat16),
        scratch_shapes=scratch,
        compiler_params=_cparams(2),
        name=f"proj_qkv_d{dil}",
    )(*args)


def _rope_tables(L):
    inv_freq = jnp.power(ROPE_THETA, -jnp.arange(0, ROT_DIM, 2, dtype=jnp.float32) / ROT_DIM)
    ang = jnp.arange(L, dtype=jnp.float32)[:, None] * inv_freq[None, :]
    cos, sin = jnp.cos(ang), jnp.sin(ang)
    ones = jnp.ones((L, HEAD_DIM - ROT_DIM), jnp.float32)
    cos_t = jnp.concatenate([cos, cos, ones], axis=1)
    sin_t = jnp.concatenate([-sin, sin, 0.0 * ones], axis=1)
    scale = 1.0 / math.sqrt(HEAD_DIM)
    return cos_t * scale, sin_t * scale, cos_t, sin_t


def _attn_kernel(q_ref, kp_ref, kc_ref, kn_ref, vp_ref, vc_ref, vn_ref, o_ref, lse_ref, kw, vw, *, n):
    tq = q_ref.shape[0]
    i = pl.program_id(1)
    kw[0:RADIUS] = kp_ref[...]
    kw[RADIUS:RADIUS + tq] = kc_ref[...]
    kw[RADIUS + tq:] = kn_ref[...]
    vw[0:RADIUS] = vp_ref[...]
    vw[RADIUS:RADIUS + tq] = vc_ref[...]
    vw[RADIUS + tq:] = vn_ref[...]
    win = ATTN_SUB + 2 * RADIUS
    row = lax.broadcasted_iota(jnp.int32, (ATTN_SUB, win), 0)
    col = lax.broadcasted_iota(jnp.int32, (ATTN_SUB, win), 1)
    band = (col >= row) & (col <= row + 2 * RADIUS)
    lane = lax.broadcasted_iota(jnp.int32, (ATTN_SUB, LANES), 1)
    nsub = tq // ATTN_SUB
    for s in range(nsub):
        q0 = i * tq + s * ATTN_SUB
        valid = band
        if s == 0:
            valid = valid & (col >= RADIUS - q0)
        if s == nsub - 1:
            valid = valid & (col < n - q0 + RADIUS)
        lse_tile = jnp.zeros((ATTN_SUB, LANES), jnp.float32)
        for h in range(N_SLOTS):
            cs = slice(h * HEAD_DIM, (h + 1) * HEAD_DIM)
            qb = q_ref[s * ATTN_SUB:(s + 1) * ATTN_SUB, cs]
            kb = kw[s * ATTN_SUB:s * ATTN_SUB + win, cs]
            vb = vw[s * ATTN_SUB:s * ATTN_SUB + win, cs]
            sc = lax.dot_general(qb, kb, (((1,), (1,)), ((), ())), preferred_element_type=jnp.float32)
            sc = jnp.where(valid, sc, -1e30)
            m = jnp.max(sc, axis=1, keepdims=True)
            p = jnp.exp(sc - m)
            l = jnp.sum(p, axis=1, keepdims=True)
            o = jnp.dot(p.astype(jnp.bfloat16), vb, preferred_element_type=jnp.float32)
            o_ref[s * ATTN_SUB:(s + 1) * ATTN_SUB, cs] = o / l
            lse_tile = jnp.where(lane == h, m + jnp.log(l), lse_tile)
        lse_ref[s * ATTN_SUB:(s + 1) * ATTN_SUB, :] = lse_tile


def _attention(qkv):
    _, S, n, W = qkv.shape
    tq = min(ATTN_TQ, n)
    r = tq // RADIUS
    last = n // RADIUS - 1

    def cur(which):
        return pl.BlockSpec((None, None, tq, W), lambda s, i: (which, s, i, 0))

    def prev(which):
        return pl.BlockSpec((None, None, RADIUS, W), lambda s, i: (which, s, jnp.maximum(i * r - 1, 0), 0))

    def nxt(which):
        return pl.BlockSpec((None, None, RADIUS, W), lambda s, i: (which, s, jnp.minimum((i + 1) * r, last), 0))

    return pl.pallas_call(
        functools.partial(_attn_kernel, n=n),
        grid=(S, n // tq),
        in_specs=[cur(0), prev(1), cur(1), nxt(1), prev(2), cur(2), nxt(2)],
        out_specs=[pl.BlockSpec((None, tq, W), lambda s, i: (s, i, 0)),
                   pl.BlockSpec((None, tq, LANES), lambda s, i: (s, i, 0))],
        out_shape=[jax.ShapeDtypeStruct((S, n, W), jnp.float32),
                   jax.ShapeDtypeStruct((S, n, LANES), jnp.float32)],
        scratch_shapes=[pltpu.VMEM((tq + 2 * RADIUS, W), jnp.bfloat16),
                        pltpu.VMEM((tq + 2 * RADIUS, W), jnp.bfloat16)],
        compiler_params=_cparams(2),
        name="attn",
    )(qkv, qkv, qkv, qkv, qkv, qkv, qkv)


def _shift_rows(u, edge_row, down):
    tl = u.shape[0]
    rolled = pltpu.roll(u, 1 if down else tl - 1, axis=0)
    r8 = lax.broadcasted_iota(jnp.int32, (SUBLANES, u.shape[1]), 0)
    if down:
        head = jnp.where(r8 == 0, edge_row, rolled[:SUBLANES])
        return jnp.concatenate([head, rolled[SUBLANES:]], axis=0)
    tail = jnp.where(r8 == SUBLANES - 1, edge_row, rolled[tl - SUBLANES:])
    return jnp.concatenate([rolled[:tl - SUBLANES], tail], axis=0)


def _hyena_pre_kernel(u_ref, up_ref, un_ref, z_ref, cw_ref, cb_ref, p_ref, g_ref, m_ref, *, n2, nblk):
    tl = u_ref.shape[0]
    i = pl.program_id(1)
    has_prev = (i > 0).astype(jnp.float32)
    has_next = (i < nblk - 1).astype(jnp.float32)
    C = HYENA_WIDTH
    rows = tl // n2
    perm = p_ref[...]

    def conv(cs):
        u = u_ref[:, cs].astype(jnp.float32)
        edge_prev = up_ref[HALO_ROWS - 1:HALO_ROWS, cs].astype(jnp.float32) * has_prev
        edge_next = un_ref[0:1, cs].astype(jnp.float32) * has_next
        u_prev = _shift_rows(u, edge_prev, True)
        u_next = _shift_rows(u, edge_next, False)
        return u_prev * cw_ref[0:1, cs] + u * cw_ref[1:2, cs] + u_next * cw_ref[2:3, cs] + cb_ref[:, cs]

    for c in range(C // LANES):
        ls = slice(c * LANES, (c + 1) * LANES)
        x0, x1, vh = (conv(slice(k * C + c * LANES, k * C + (c + 1) * LANES)) for k in range(3))
        z = z_ref[:, ls].astype(jnp.float32)
        m_hi, m_lo = _split_bf16(x0 * (z * _sigmoid(z)))
        stack = jnp.concatenate([(vh * x1).astype(jnp.bfloat16), m_hi, m_lo], axis=1)
        res = jnp.dot(perm, stack, preferred_element_type=jnp.float32)
        g = res[:, :LANES]
        m = res[:, LANES:2 * LANES] + res[:, 2 * LANES:]
        for r in range(n2):
            g_ref[r, :, ls] = g[r * rows:(r + 1) * rows].astype(g_ref.dtype)
            m_ref[r, :, ls] = m[r * rows:(r + 1) * rows]


def _hyena_pre(u_hy, gates, conv_w, conv_b, n2):
    B, L, C3 = u_hy.shape
    C = HYENA_WIDTH
    tl = HY_TL
    nblk = L // tl
    per = tl // HALO_ROWS
    last_halo = L // HALO_ROWS - 1
    out_spec = pl.BlockSpec((None, n2, tl // n2, C), lambda b, i: (b, 0, i, 0))
    out_sds = jax.ShapeDtypeStruct((B, n2, L // n2, C), jnp.float32)
    g_sds = jax.ShapeDtypeStruct(out_sds.shape, jnp.bfloat16 if (tl // n2) % BF16_TILE_ROWS == 0 else jnp.float32)
    return pl.pallas_call(
        functools.partial(_hyena_pre_kernel, n2=n2, nblk=nblk),
        grid=(B, nblk),
        in_specs=[pl.BlockSpec((None, tl, C3), lambda b, i: (b, i, 0)),
                  pl.BlockSpec((None, HALO_ROWS, C3), lambda b, i: (b, jnp.maximum(i * per - 1, 0), 0)),
                  pl.BlockSpec((None, HALO_ROWS, C3), lambda b, i: (b, jnp.minimum((i + 1) * per, last_halo), 0)),
                  pl.BlockSpec((None, tl, C), lambda b, i: (b, i, 1)),
                  pl.BlockSpec((3, C3), lambda b, i: (0, 0)),
                  pl.BlockSpec((1, C3), lambda b, i: (0, 0)),
                  pl.BlockSpec((tl, tl), lambda b, i: (0, 0), pipeline_mode=pl.Buffered(1))],
        out_specs=[out_spec, out_spec],
        out_shape=[g_sds, out_sds],
        compiler_params=_cparams(2),
        name="hyena_pre",
    )(u_hy, u_hy, u_hy, gates, conv_w, conv_b, _class_major_onehot(tl, n2))


POS_VALID, POS_ZERO = FILTER_EMB, FILTER_EMB + 1


def _dot3(a, wh_ref, wl_ref):
    ah, al = _split_bf16(a)
    wh = wh_ref[...]
    return (jnp.dot(ah, wh, preferred_element_type=jnp.float32)
            + jnp.dot(al, wh, preferred_element_type=jnp.float32)
            + jnp.dot(ah, wl_ref[...], preferred_element_type=jnp.float32))


def _filter_kernel(p_ref, w1h, w1l, b1_ref, w2h, w2l, b2_ref, w3h, w3l, b3_ref, w4fh, w4fl, w4bh, w4bl,
                   fr_ref, dl_ref, sk_ref, o_ref):
    p = p_ref[...]
    fr = fr_ref[...]
    p2 = jnp.concatenate([p[:DFT_HALF], p[DFT_HALF:]], axis=1)
    hdn = jnp.sin(fr * (_dot3(p2, w1h, w1l) + b1_ref[...]))
    hdn = jnp.sin(fr * (_dot3(hdn, w2h, w2l) + b2_ref[...]))
    hdn = jnp.sin(fr * (_dot3(hdn, w3h, w3l) + b3_ref[...]))
    scale = jnp.exp(-p[:, 0:1] * dl_ref[...]) * p[:, POS_VALID:POS_VALID + 1]
    skip = p[:, POS_ZERO:POS_ZERO + 1] * sk_ref[...]
    for d, (wh, wl) in enumerate(((w4fh, w4fl), (w4bh, w4bl))):
        rs = slice(d * DFT_HALF, (d + 1) * DFT_HALF)
        o_ref[rs, :] = (_dot3(hdn, wh, wl) * scale[rs] + skip[rs]).astype(o_ref.dtype)


def _filter_lag_table(L, n2):
    bands = (FILTER_EMB - 1) // 2
    f32 = np.float32
    n = np.arange(2 * L).reshape(2 * L // n2, n2).T.reshape(2 * L)
    lag = np.where(n < L, n, 2 * L - n)
    idx = np.where(n == L, 0, lag).astype(f32)[:, None]
    t = np.where(idx == L - 1, f32(1.0), idx * f32(1.0 / (L - 1))).astype(f32)
    w = (f32(2.0 * math.pi) * idx / f32(L)).astype(f32)
    f = np.linspace(1e-4, bands - 1, bands, dtype=f32)[None, :]
    arg = (f * w).astype(f32)
    flags = np.stack([n != L, n == 0], axis=1).astype(f32)
    tab = np.concatenate([t, np.cos(arg), -np.sin(arg), flags], axis=1).astype(f32)
    return jnp.pad(jnp.asarray(tab), ((0, 0), (0, LANES - tab.shape[1])))


def _filter(L, n2, lp):
    C = HYENA_WIDTH
    tab = _filter_lag_table(L, n2)
    H = FILTER_HIDDEN
    assert 2 * H == LANES
    zeros = lambda r, c: jnp.zeros((r, c), jnp.float32)
    blockdiag = lambda w: jnp.block([[w, zeros(*w.shape)], [zeros(*w.shape), w]])
    w1p = jnp.concatenate([lp["filt_w1"], zeros(LANES - FILTER_EMB, H)], axis=0)
    w4 = lp["filt_w4"]
    w4f = jnp.concatenate([w4[:, :C], zeros(H, C)], axis=0)
    w4b = jnp.concatenate([zeros(H, C), w4[:, C:]], axis=0)
    twice = lambda v: jnp.concatenate([v, v])[None, :]
    min_decay = math.log(DECAY_TARGET) / DECAY_FAST_PCT
    max_decay = math.log(DECAY_TARGET) / DECAY_SLOW_PCT
    deltas = jnp.abs(jnp.linspace(min_decay, max_decay, C, dtype=jnp.float32))[None, :]
    full = lambda a: pl.BlockSpec(a.shape, lambda i: (0,) * a.ndim)
    args = [*_split_bf16(blockdiag(w1p)), twice(lp["filt_b1"]), *_split_bf16(blockdiag(lp["filt_w2"])),
            twice(lp["filt_b2"]), *_split_bf16(blockdiag(lp["filt_w3"])), twice(lp["filt_b3"]),
            *_split_bf16(w4f), *_split_bf16(w4b), twice(lp["filt_freq"]), deltas, lp["hyena_skip"][None, :]]
    k = pl.pallas_call(
        _filter_kernel,
        grid=(n2,),
        in_specs=[pl.BlockSpec((DFT_N1, LANES), lambda i: (i, 0))] + [full(a) for a in args],
        out_specs=pl.BlockSpec((DFT_N1, C), lambda i: (i, 0)),
        out_shape=jax.ShapeDtypeStruct((2 * L, C), jnp.bfloat16),
        compiler_params=_cparams(1),
        name="filter",
    )(tab, *args)
    return k.reshape(n2, DFT_N1, C)


def _dft_tables(n2):
    N = DFT_N1 * n2
    k1 = np.arange(DFT_ROWS)
    live = (k1 <= DFT_HALF).astype(np.float64)[:, None]
    a = 2.0 * np.pi * ((k1[:, None] * np.arange(DFT_N1)[None, :]) % DFT_N1) / DFT_N1
    b = 2.0 * np.pi * (k1[:, None] * np.arange(n2)[None, :]) / N
    fwd = np.concatenate([np.cos(a) * live, -np.sin(a) * live], axis=0)
    weight = (np.where((k1 == 0) | (k1 == DFT_HALF), 1.0, 2.0) / N)[:, None]
    inv = np.concatenate([np.cos(a) * live * weight, -np.sin(a) * live * weight], axis=0).T[:DFT_HALF]
    tw = np.stack([np.cos(b), np.sin(b)], axis=2).reshape(DFT_ROWS // SUBLANES, SUBLANES, 2 * n2)
    return jnp.asarray(fwd, jnp.bfloat16), jnp.asarray(inv, jnp.bfloat16), jnp.asarray(tw, jnp.float32)


def _twiddle_factors(tw_c, n2):
    bc = lambda col: jnp.broadcast_to(tw_c[:, col:col + 1], (SUBLANES, LANES))
    return [None if r == 0 else (bc(2 * r), bc(2 * r + 1)) for r in range(n2)]


def _twiddle(z, factors, sign):
    if factors is None:
        return z
    re, im = z
    cb, sb = factors
    if sign < 0:
        return re * cb + im * sb, im * cb - re * sb
    return re * cb - im * sb, re * sb + im * cb


def _filter_spectrum_kernel(k_ref, f_ref, tw_ref, o_ref, *, n2, slabs):
    nj = n2 // slabs
    j = pl.program_id(1)
    for s in range(slabs):
        o_ref[j * slabs + s] = jnp.dot(f_ref[...], k_ref[s], preferred_element_type=jnp.float32)

    @pl.when(j == nj - 1)
    def _across_slabs():
        def body(c, carry):
            re = pl.ds(pl.multiple_of(c * SUBLANES, SUBLANES), SUBLANES)
            im = pl.ds(pl.multiple_of(c * SUBLANES + DFT_ROWS, SUBLANES), SUBLANES)
            tw = _twiddle_factors(tw_ref[c], n2)
            for t in range(o_ref.shape[2] // LANES):
                ls = slice(t * LANES, (t + 1) * LANES)
                ys = _fft_list([_twiddle((o_ref[r, re, ls], o_ref[r, im, ls]), tw[r], -1)
                                for r in range(n2)], -1)
                for k2 in range(n2):
                    o_ref[k2, re, ls] = ys[k2][0]
                    o_ref[k2, im, ls] = ys[k2][1]
            return carry

        lax.fori_loop(0, DFT_ROWS // SUBLANES, body, 0, unroll=11)


def _const_spec(a):
    return pl.BlockSpec(a.shape, lambda *_: (0,) * a.ndim, pipeline_mode=pl.Buffered(1))


def _filter_spectrum(k, fwd, tw, slabs=16):
    n2, K, C = k.shape
    slabs = min(slabs, n2)
    tc = 2 * LANES
    return pl.pallas_call(
        functools.partial(_filter_spectrum_kernel, n2=n2, slabs=slabs),
        grid=(C // tc, n2 // slabs),
        in_specs=[pl.BlockSpec((slabs, K, tc), lambda c, j: (j, 0, c)), _const_spec(fwd), _const_spec(tw)],
        out_specs=pl.BlockSpec((n2, DFT_M, tc), lambda c, j: (0, 0, c), pipeline_mode=pl.Buffered(1)),
        out_shape=jax.ShapeDtypeStruct((n2, DFT_M, C), jnp.float32),
        compiler_params=pltpu.CompilerParams(dimension_semantics=("parallel", "arbitrary"),
                                             vmem_limit_bytes=VMEM_LIMIT),
        name="filter_spectrum",
    )(k, fwd, tw)


def _fft_list(xs, sign):
    n = len(xs)
    if n == 1:
        return xs
    even = _fft_list(xs[0::2], sign)
    odd = _fft_list(xs[1::2], sign)
    out = [None] * n
    for k in range(n // 2):
        er, ei = even[k]
        pr, pi = odd[k]
        if k == 0:
            tr, ti = pr, pi
        elif 4 * k == n:
            tr, ti = (pi, -pr) if sign < 0 else (-pi, pr)
        else:
            ang = sign * 2.0 * math.pi * k / n
            wr, wi = math.cos(ang), math.sin(ang)
            tr, ti = pr * wr - pi * wi, pr * wi + pi * wr
        out[k] = (er + tr, ei + ti)
        out[k + n // 2] = (er - tr, ei - ti)
    return out


def _long_conv_kernel(x_ref, f_ref, g_ref, tw_ref, gate_ref, kf_ref, o_ref, spec, *, n2, slabs, resident):
    nj = n2 // slabs
    j = pl.program_id(2)
    tc = spec.shape[2]
    fwd_base = j * slabs if resident else 0
    inv_base = (j - nj) * slabs if resident else 0

    @pl.when(j < nj)
    def _forward():
        for s in range(slabs):
            spec[j * slabs + s] = jnp.dot(f_ref[...], x_ref[fwd_base + s].astype(jnp.bfloat16),
                                          preferred_element_type=jnp.float32)

    @pl.when(j == nj - 1)
    def _across_slabs():
        def body(c, carry):
            re = pl.ds(pl.multiple_of(c * SUBLANES, SUBLANES), SUBLANES)
            im = pl.ds(pl.multiple_of(c * SUBLANES + DFT_ROWS, SUBLANES), SUBLANES)
            tw = _twiddle_factors(tw_ref[c], n2)
            for t in range(tc // LANES):
                ls = slice(t * LANES, (t + 1) * LANES)
                ys = _fft_list([_twiddle((spec[r, re, ls], spec[r, im, ls]), tw[r], -1)
                                for r in range(n2)], -1)
                prod = []
                for k2 in range(n2):
                    yr, yi = ys[k2]
                    fr, fi = kf_ref[k2, re, ls], kf_ref[k2, im, ls]
                    prod.append((yr * fr - yi * fi, yr * fi + yi * fr))
                vs = _fft_list(prod, +1)
                for r in range(n2):
                    vr, vi = _twiddle(vs[r], tw[r], +1)
                    spec[r, re, ls] = vr
                    spec[r, im, ls] = vi
            return carry

        lax.fori_loop(0, DFT_ROWS // SUBLANES, body, 0, unroll=11)

    @pl.when(j >= nj)
    def _inverse():
        for s in range(slabs):
            v = spec[(j - nj) * slabs + s].astype(jnp.bfloat16)
            o_ref[inv_base + s] = (jnp.dot(g_ref[...], v, preferred_element_type=jnp.float32)
                                   * gate_ref[inv_base + s])


def _long_conv(x, gate, fwd, inv, tw, kf, slabs=16):
    B, n2, K, C = x.shape
    slabs = min(slabs, n2)
    nj = n2 // slabs
    spectra = lambda tc: 2 * n2 * DFT_M * tc * 4
    tc = 2 * LANES if spectra(2 * LANES) <= LONG_CONV_VMEM else LANES
    whole = 2 * n2 * K * tc * (4 + 4 + x.dtype.itemsize)
    resident = spectra(tc) + whole <= LONG_CONV_RESIDENT_VMEM
    fwd_j = lambda j: jnp.minimum(j, nj - 1)
    inv_j = lambda j: jnp.maximum(j - nj, 0)
    if resident:
        io_spec = pl.BlockSpec((None, n2, K, tc), lambda c, b, j: (b, 0, 0, c))
        x_spec = io_spec
    else:
        io_spec = pl.BlockSpec((None, slabs, K, tc), lambda c, b, j: (b, inv_j(j), 0, c))
        x_spec = pl.BlockSpec((None, slabs, K, tc), lambda c, b, j: (b, fwd_j(j), 0, c))
    f_spec = pl.BlockSpec((DFT_M, K), lambda c, b, j: (0, 0), pipeline_mode=pl.Buffered(1))
    return pl.pallas_call(
        functools.partial(_long_conv_kernel, n2=n2, slabs=slabs, resident=resident),
        grid=(C // tc, B, 2 * nj),
        in_specs=[x_spec, f_spec, _const_spec(inv), _const_spec(tw), io_spec,
                  pl.BlockSpec((n2, DFT_M, tc), lambda c, b, j: (0, 0, c), pipeline_mode=pl.Buffered(1))],
        out_specs=io_spec,
        out_shape=jax.ShapeDtypeStruct((B, n2, K, C), jnp.float32),
        scratch_shapes=[pltpu.VMEM((n2, DFT_M, tc), jnp.float32)],
        compiler_params=pltpu.CompilerParams(dimension_semantics=("parallel", "parallel", "arbitrary"),
                                             vmem_limit_bytes=VMEM_LIMIT),
        name="long_conv",
    )(x, fwd, inv, tw, gate, kf)


def _regroup_to_tokens(blk_ref, scr, dil, width):
    m = blk_ref.shape[1]
    for c in range(width // LANES):
        for r in range(dil):
            scr[c, pl.ds(r, m, stride=dil), :] = blk_ref[r, :, c * LANES:(c + 1) * LANES]


def _merge_kernel(o0_ref, l0_ref, o1_ref, l1_ref, o2_ref, l2_ref, hy_ref, za_ref, ga_ref, gh_ref,
                  wa_ref, wh_ref, out_ref, so1, so2, sl1, sl2, shy, attn_scr, hy_scr, *, n2):
    d1, d2 = ATTN_GROUPS[1][1], ATTN_GROUPS[2][1]
    _regroup_to_tokens(o1_ref, so1, d1, ATTN_WIDTH)
    _regroup_to_tokens(o2_ref, so2, d2, ATTN_WIDTH)
    _regroup_to_tokens(l1_ref, sl1, d1, LANES)
    _regroup_to_tokens(l2_ref, sl2, d2, LANES)
    _regroup_to_tokens(hy_ref, shy, n2, HYENA_WIDTH)
    tm = out_ref.shape[0]
    for half in range(2):
        rs = slice(half * tm // 2, (half + 1) * tm // 2)
        l0, l1, l2 = l0_ref[rs, :], sl1[0, rs, :], sl2[0, rs, :]
        mx = jnp.maximum(jnp.maximum(l0, l1), l2)
        e0, e1, e2 = jnp.exp(l0 - mx), jnp.exp(l1 - mx), jnp.exp(l2 - mx)
        inv = 1.0 / (e0 + e1 + e2)
        a0, a1, a2 = e0 * inv, e1 * inv, e2 * inv
        for h in range(N_SLOTS):
            cs = slice(h * HEAD_DIM, (h + 1) * HEAD_DIM)
            mix = (a0[:, h:h + 1] * o0_ref[rs, cs] + a1[:, h:h + 1] * so1[h, rs, :]
                   + a2[:, h:h + 1] * so2[h, rs, :])
            z = za_ref[rs, cs].astype(jnp.float32)
            attn_scr[rs, cs] = (mix * (z * _sigmoid(z))).astype(jnp.bfloat16)
            hy_scr[rs, cs] = shy[h, rs, :].astype(jnp.bfloat16)
        br_a = jnp.dot(attn_scr[rs, :], wa_ref[...], preferred_element_type=jnp.float32)
        br_h = jnp.dot(hy_scr[rs, :], wh_ref[...], preferred_element_type=jnp.float32)
        merged = (_sigmoid(ga_ref[rs, :].astype(jnp.float32)) * br_a
                  + _sigmoid(gh_ref[rs, :].astype(jnp.float32)) * br_h)
        out_ref[rs, :] = merged.astype(out_ref.dtype)


def _merge(outs, lses, hyena, gates, w_ba, w_bh, n2):
    B, d0, L, W = outs[0].shape
    D = w_ba.shape[1]
    tm = MERGE_TM
    d1, d2 = ATTN_GROUPS[1][1], ATTN_GROUPS[2][1]

    def cls(d, width):
        return pl.BlockSpec((None, d, tm // d, width), lambda b, i: (b, 0, i, 0))

    in_specs = [pl.BlockSpec((None, None, tm, W), lambda b, i: (b, 0, i, 0)),
                pl.BlockSpec((None, None, tm, LANES), lambda b, i: (b, 0, i, 0)),
                cls(d1, W), cls(d1, LANES), cls(d2, W), cls(d2, LANES), cls(n2, HYENA_WIDTH),
                pl.BlockSpec((None, tm, W), lambda b, i: (b, i, 0)),
                pl.BlockSpec((None, tm, D), lambda b, i: (b, i, 1)),
                pl.BlockSpec((None, tm, D), lambda b, i: (b, i, 2)),
                pl.BlockSpec(w_ba.shape, lambda b, i: (0, 0), pipeline_mode=pl.Buffered(1)),
                pl.BlockSpec(w_bh.shape, lambda b, i: (0, 0), pipeline_mode=pl.Buffered(1))]
    wt = W // LANES
    return pl.pallas_call(
        functools.partial(_merge_kernel, n2=n2),
        grid=(B, L // tm),
        in_specs=in_specs,
        out_specs=pl.BlockSpec((None, tm, D), lambda b, i: (b, i, 0)),
        out_shape=jax.ShapeDtypeStruct((B, L, D), jnp.bfloat16),
        scratch_shapes=[pltpu.VMEM((wt, tm, LANES), jnp.float32), pltpu.VMEM((wt, tm, LANES), jnp.float32),
                        pltpu.VMEM((1, tm, LANES), jnp.float32), pltpu.VMEM((1, tm, LANES), jnp.float32),
                        pltpu.VMEM((wt, tm, LANES), jnp.float32),
                        pltpu.VMEM((tm, W), jnp.bfloat16), pltpu.VMEM((tm, HYENA_WIDTH), jnp.bfloat16)],
        compiler_params=_cparams(2),
        name="merge",
    )(outs[0], lses[0], outs[1], lses[1], outs[2], lses[2], hyena, gates, gates, gates, w_ba, w_bh)


def _out_kernel(m_ref, w_ref, x_ref, g_ref, o_ref, *, final_norm):
    y = x_ref[...] + jnp.dot(m_ref[...], w_ref[...], preferred_element_type=jnp.float32)
    if final_norm:
        ms = jnp.mean(y * y, axis=-1, keepdims=True)
        y = y * lax.rsqrt(ms + RMS_EPS) * g_ref[...]
    o_ref[...] = y


def _out_proj(merged, w_out, x, final_gain, final_norm):
    B, L, D = x.shape
    tm = MERGE_TM
    tok = pl.BlockSpec((None, tm, D), lambda b, i: (b, i, 0))
    return pl.pallas_call(
        functools.partial(_out_kernel, final_norm=final_norm),
        grid=(B, L // tm),
        in_specs=[tok, pl.BlockSpec(w_out.shape, lambda b, i: (0, 0), pipeline_mode=pl.Buffered(1)),
                  tok, pl.BlockSpec((1, D), lambda b, i: (0, 0))],
        out_specs=tok,
        out_shape=jax.ShapeDtypeStruct((B, L, D), jnp.float32),
        compiler_params=_cparams(2),
        name="out_proj",
    )(merged, w_out, x, final_gain)


def _prep_layer(i, norm_gain, w_in, conv_w, conv_b, filt_w1, filt_b1, filt_w2, filt_b2, filt_w3, filt_b3,
                filt_w4, filt_freq, hyena_skip, w_branch_attn, w_branch_hyena, w_out):
    bf = jnp.bfloat16
    return dict(
        gain=norm_gain[i][None, :],
        w_in=w_in[i].astype(bf),
        conv_w=conv_w[i], conv_b=conv_b[i][None, :],
        filt_w1=filt_w1[i], filt_b1=filt_b1[i], filt_w2=filt_w2[i], filt_b2=filt_b2[i],
        filt_w3=filt_w3[i], filt_b3=filt_b3[i], filt_w4=filt_w4[i], filt_freq=filt_freq[i],
        hyena_skip=hyena_skip[i],
        w_ba=w_branch_attn[i].astype(bf), w_bh=w_branch_hyena[i].astype(bf), w_out=w_out[i].astype(bf))


def _rope_tables_by_dilation(L):
    base = _rope_tables(L)
    return {dil: tuple(_class_major(t, PROJ_TM, dil) for t in base) if dil >= PERM_MIN_DIL else base
            for _, dil in ATTN_GROUPS}


def _layer(x, lp, final_gain, final_norm, rope):
    B, L, D = x.shape
    assert L % (2 * DFT_HALF) == 0 and L % (2 * PROJ_TM) == 0 and D == 2 * W_BLOCK
    n2 = L // DFT_HALF

    outs, lses = [], []
    for g, (_, dil) in enumerate(ATTN_GROUPS):
        qkv = _proj_qkv(x, lp["gain"], lp["w_in"], rope[dil], dil, g)
        n = L // dil
        o, lse = _attention(qkv.reshape(3, B * dil, n, ATTN_WIDTH))
        outs.append(o.reshape(B, dil, n, ATTN_WIDTH))
        lses.append(lse.reshape(B, dil, n, LANES))

    gate_cols = (lambda s: jnp.where(s == 0, GATE_BLOCKS[0], GATE_BLOCKS[2] + 2 * (s - 1)),
                 lambda s: jnp.where(s == 0, GATE_BLOCKS[1], GATE_BLOCKS[2] + 2 * (s - 1) + 1))
    gates = _proj_plain(x, lp["gain"], lp["w_in"], gate_cols, 3, jnp.bfloat16)
    uhy_cols = tuple(functools.partial(lambda s, c: c, c=UHY_BLOCK + k) for k in range(3))
    u_hy = _proj_plain(x, lp["gain"], lp["w_in"], uhy_cols, 1, jnp.bfloat16)
    gated, mult = _hyena_pre(u_hy, gates, lp["conv_w"], lp["conv_b"], n2)
    fwd_mat, inv_mat, tw = _dft_tables(n2)
    kf = _filter_spectrum(_filter(L, n2, lp), fwd_mat, tw)
    hyena = _long_conv(gated, mult, fwd_mat, inv_mat, tw, kf)

    merged = _merge(outs, lses, hyena, gates, lp["w_ba"], lp["w_bh"], n2)
    return _out_proj(merged, lp["w_out"], x, final_gain, final_norm)


def kernel(x_prompt, x_sample, norm_gain, w_in, conv_w, conv_b, filt_w1, filt_b1, filt_w2, filt_b2, filt_w3, filt_b3, filt_w4, filt_freq, hyena_skip, w_branch_attn, w_branch_hyena, w_out, final_gain):
    depth = w_in.shape[0]
    layers = [_prep_layer(i, norm_gain, w_in, conv_w, conv_b, filt_w1, filt_b1, filt_w2, filt_b2, filt_w3,
                          filt_b3, filt_w4, filt_freq, hyena_skip, w_branch_attn, w_branch_hyena, w_out)
              for i in range(depth)]
    fg = final_gain[None, :]
    rope = _rope_tables_by_dilation(max(x_prompt.shape[1], x_sample.shape[1]))

    def trunk(x):
        for i, lp in enumerate(layers):
            x = _layer(x, lp, fg, final_norm=(i == depth - 1), rope=rope)
        return x

    return (trunk(x_prompt), trunk(x_sample))
```
